```python
import math
import jax, jax.numpy as jnp
from jax import lax
import numpy as np

D_MODEL = 1024
BATCH = 2
SEQ = 8192
DEPTH = 2

N_A_LAYERS = DEPTH // 2
N_B_LAYERS = DEPTH - N_A_LAYERS
HEAD_DIM = 64
N_HEADS = D_MODEL // HEAD_DIM
DECAY_LORA = 64
AAA_LORA = 64
GATE_LORA = 160
N_SHIFT_MIX = 6
GN_EPS = 64e-5
RMS_EPS = 1e-6
SB_BLOCK = 128
N_GROUPS = 4
EXPERTS_PER_GROUP = 4
N_EXPERTS = N_GROUPS * EXPERTS_PER_GROUP
TOP_K_INNER = 2
D_EXPERT = 256

kernel_name = "yoco_rwkv7_stickbreak_hmoe"


def rmsnorm(x, g):
    xf = x.astype(jnp.float32)
    y = xf * lax.rsqrt(jnp.mean(xf * xf, axis=-1, keepdims=True) + RMS_EPS)
    return (y * g.astype(jnp.float32)).astype(x.dtype)


def token_shift(x):
    return jnp.pad(x[:, :-1], ((0, 0), (1, 0), (0, 0)))


def wkv7_scan(r, decay, k, v, a, b):
    Bsz, T, H, N = r.shape

    def step(S, inp):
        r_t, w_t, k_t, v_t, a_t, b_t = inp
        sa = jnp.einsum('bhij,bhj->bhi', S, a_t)
        S = (S * w_t[:, :, None, :] + sa[..., None] * b_t[:, :, None, :]
             + v_t[..., None] * k_t[:, :, None, :])
        y = jnp.einsum('bhij,bhj->bhi', S, r_t)
        return S, y

    xs = tuple(jnp.moveaxis(t, 1, 0) for t in (r, decay, k, v, a, b))
    S0 = jnp.zeros((Bsz, H, N, N), jnp.float32)
    _, ys = lax.scan(step, S0, xs)
    return jnp.moveaxis(ys, 0, 1)


def rwkv7_time_mix(x, mix, w_rkv, w0, w1, w2, a0, a1, a2, g1, g2, k_k, k_a, r_k,
                   lnx_w, lnx_b, w_out):
    Bsz, T, C = x.shape
    H, N = N_HEADS, HEAD_DIM
    xx = token_shift(x) - x
    xs = x[None] + xx[None] * mix[:, None, None, :]
    rkv = jnp.einsum('nbtc,ncd->nbtd', xs[:3], w_rkv)
    r, k, v = rkv[0], rkv[1], rkv[2]
    xw, xa, xg = xs[3], xs[4], xs[5]
    w = -jax.nn.softplus(-(w0 + jnp.tanh(xw @ w1) @ w2)) - 0.5
    a = jax.nn.sigmoid(a0 + (xa @ a1) @ a2)
    g = jax.nn.sigmoid(xg @ g1) @ g2

    heads = lambda t: t.reshape(Bsz, T, H, N).astype(jnp.float32)
    kk = heads(k * k_k)
    kk = kk / jnp.maximum(jnp.sqrt(jnp.sum(kk * kk, axis=-1, keepdims=True)), 1e-12)
    k = k * (1.0 + (a - 1.0) * k_a)
    rh, kh, vh, ah = heads(r), heads(k), heads(v), heads(a)
    decay = jnp.exp(-jnp.exp(heads(w)))
    y = wkv7_scan(rh, decay, kh, vh, -kk, kk * ah)

    mu = jnp.mean(y, axis=-1, keepdims=True)
    var = jnp.mean(jnp.square(y - mu), axis=-1, keepdims=True)
    y = ((y - mu) * lax.rsqrt(var + GN_EPS)).reshape(Bsz, T, C)
    y = y * lnx_w.astype(jnp.float32) + lnx_b.astype(jnp.float32)
    bonus = jnp.sum(rh * kh * r_k.astype(jnp.float32), axis=-1, keepdims=True) * vh
    y = (y + bonus.reshape(Bsz, T, C)).astype(x.dtype)
    return (y * g) @ w_out


def stick_breaking_attention(q, k, v):
    Bsz, H, S, N = q.shape
    nb = S // SB_BLOCK
    scale = 1.0 / math.sqrt(N)
    qb = q.reshape(Bsz, H, nb, SB_BLOCK, N).transpose(2, 0, 1, 3, 4)
    kpos = jnp.arange(S)

    def block(args):
        i, q_blk = args
        qpos = i * SB_BLOCK + jnp.arange(SB_BLOCK)
        z = jnp.einsum('bhqd,bhkd->bhqk', q_blk, k).astype(jnp.float32) * scale
        mask = kpos[None, :] < qpos[:, None]
        u = jnp.where(mask, jax.nn.softplus(z), 0.0)
        suffix = lax.cumsum(u, axis=3, reverse=True) - u
        att = jnp.where(mask, jnp.exp(jax.nn.log_sigmoid(z) - suffix), 0.0)
        return jnp.einsum('bhqk,bhkd->bhqd', att.astype(v.dtype), v)

    o = lax.map(block, (jnp.arange(nb), qb))
    return o.transpose(1, 2, 0, 3, 4).reshape(Bsz, H, S, N)


def hier_moe(x, w_group, b_group, w_inner, b_inner, w_gate, w_up, w_down):
    gl = jnp.einsum('btd,dg->btg', x, w_group).astype(jnp.float32) + b_group.astype(jnp.float32)
    pg = jax.nn.softmax(gl, axis=-1)
    gval, gidx = lax.top_k(pg, 1)
    il = jnp.einsum('btd,dge->btge', x, w_inner).astype(jnp.float32) + b_inner.astype(jnp.float32)
    il_sel = jnp.einsum('btge,btg->bte', il, jax.nn.one_hot(gidx[..., 0], N_GROUPS, dtype=jnp.float32))
    pi = jax.nn.softmax(il_sel, axis=-1)
    ival, iidx = lax.top_k(pi, TOP_K_INNER)
    weight = gval * ival / jnp.sum(ival, axis=-1, keepdims=True)
    eidx = gidx * EXPERTS_PER_GROUP + iidx
    gates = jnp.sum(jax.nn.one_hot(eidx, N_EXPERTS, dtype=jnp.float32) * weight[..., None], axis=-2)
    h = (jax.nn.silu(jnp.einsum('btd,edf->btef', x, w_gate))
         * jnp.einsum('btd,edf->btef', x, w_up))
    h = h * gates.astype(h.dtype)[..., None]
    return jnp.einsum('btef,efd->btd', h, w_down)


def setup_inputs(seed: int = 0) -> dict:
    key = jax.random.key(seed)
    ks = iter(jax.random.split(key, 48))
    C, H, N = D_MODEL, N_HEADS, HEAD_DIM
    nA, nB = N_A_LAYERS, N_B_LAYERS
    G, E, NE, F = N_GROUPS, EXPERTS_PER_GROUP, N_EXPERTS, D_EXPERT

    def nrm(shape, scale):
        return jax.random.normal(next(ks), shape, jnp.float32) * scale

    def unif(shape, lo, hi):
        return jax.random.uniform(next(ks), shape, jnp.float32, lo, hi)

    return {
        "x": nrm((BATCH, SEQ, C), 1.0),
        "norm_mix": 1.0 + nrm((DEPTH, C), 0.02),
        "norm_ffn": 1.0 + nrm((DEPTH, C), 0.02),
        "norm_kv": 1.0 + nrm((C,), 0.02),
        "norm_final": 1.0 + nrm((C,), 0.02),
        "rw_mix": unif((nA, N_SHIFT_MIX, C), 0.0, 1.0),
        "rw_w_rkv": nrm((nA, 3, C, C), C ** -0.5),
        "rw_w0": unif((nA, C), -5.0, -1.0),
        "rw_w1": nrm((nA, C, DECAY_LORA), C ** -0.5),
        "rw_w2": nrm((nA, DECAY_LORA, C), 0.1 * DECAY_LORA ** -0.5),
        "rw_a0": nrm((nA, C), 0.1),
        "rw_a1": nrm((nA, C, AAA_LORA), C ** -0.5),
        "rw_a2": nrm((nA, AAA_LORA, C), 0.1 * AAA_LORA ** -0.5),
        "rw_g1": nrm((nA, C, GATE_LORA), C ** -0.5),
        "rw_g2": nrm((nA, GATE_LORA, C), GATE_LORA ** -0.5),
        "rw_k_k": 0.85 + nrm((nA, C), 0.05),
        "rw_k_a": 1.0 + nrm((nA, C), 0.05),
        "rw_r_k": nrm((nA, H, N), 0.1),
        "rw_lnx_w": 1.0 + nrm((nA, C), 0.02),
        "rw_lnx_b": nrm((nA, C), 0.02),
        "rw_w_out": nrm((nA, C, C), C ** -0.5),
        "w_kv": nrm((C, 2 * C), C ** -0.5),
        "sb_w_q": nrm((nB, C, C), C ** -0.5),
        "sb_w_out": nrm((nB, C, C), C ** -0.5),
        "moe_w_group": nrm((DEPTH, C, G), C ** -0.5),
        "moe_b_group": nrm((DEPTH, G), 0.01),
        "moe_w_inner": nrm((DEPTH, C, G, E), C ** -0.5),
        "moe_b_inner": nrm((DEPTH, G, E), 0.01),
        "moe_w_gate": nrm((DEPTH, NE, C, F), C ** -0.5),
        "moe_w_up": nrm((DEPTH, NE, C, F), C ** -0.5),
        "moe_w_down": nrm((DEPTH, NE, F, C), F ** -0.5),
    }


def reference(x, norm_mix, norm_ffn, norm_kv, norm_final,
              rw_mix, rw_w_rkv, rw_w0, rw_w1, rw_w2, rw_a0, rw_a1, rw_a2, rw_g1, rw_g2,
              rw_k_k, rw_k_a, rw_r_k, rw_lnx_w, rw_lnx_b, rw_w_out,
              w_kv, sb_w_q, sb_w_out,
              moe_w_group, moe_b_group, moe_w_inner, moe_b_inner,
              moe_w_gate, moe_w_up, moe_w_down):
    Bsz, S, C = x.shape
    H, N = N_HEADS, HEAD_DIM
    h = x
    k_shared = None
    v_shared = None
    for l in range(DEPTH):
        hn = rmsnorm(h, norm_mix[l])
        if l < N_A_LAYERS:
            i = l
            h = h + rwkv7_time_mix(hn, rw_mix[i], rw_w_rkv[i], rw_w0[i], rw_w1[i], rw_w2[i],
                                   rw_a0[i], rw_a1[i], rw_a2[i], rw_g1[i], rw_g2[i],
                                   rw_k_k[i], rw_k_a[i], rw_r_k[i], rw_lnx_w[i], rw_lnx_b[i],
                                   rw_w_out[i])
        else:
            j = l - N_A_LAYERS
            q = (hn @ sb_w_q[j]).reshape(Bsz, S, H, N).transpose(0, 2, 1, 3)
            o = stick_breaking_attention(q, k_shared, v_shared)
            h = h + o.transpose(0, 2, 1, 3).reshape(Bsz, S, C) @ sb_w_out[j]
        h = h + hier_moe(rmsnorm(h, norm_ffn[l]), moe_w_group[l], moe_b_group[l],
                         moe_w_inner[l], moe_b_inner[l], moe_w_gate[l], moe_w_up[l],
                         moe_w_down[l])
        if l == N_A_LAYERS - 1:
            kv = (rmsnorm(h, norm_kv) @ w_kv).reshape(Bsz, S, 2, H, N)
            k_shared = kv[:, :, 0].transpose(0, 2, 1, 3)
            v_shared = kv[:, :, 1].transpose(0, 2, 1, 3)
    return rmsnorm(h, norm_final)
```

```python
import functools

import jax
import jax.numpy as jnp
from jax import lax
from jax.experimental import pallas as pl
from jax.experimental.pallas import tpu as pltpu

F32 = jnp.float32
BF16 = jnp.bfloat16

LANES = 128
HEAD_DIM = 64
HEADS_PER_BLOCK = LANES // HEAD_DIM
RMS_EPS = 1e-6
GN_EPS = 64e-5
N_GROUPS = 4
EXPERTS_PER_GROUP = 4
N_EXPERTS = N_GROUPS * EXPERTS_PER_GROUP
WKV_CHUNK = 64
VMEM_LIMIT = 56 * 1024 * 1024

HIGHEST = lax.Precision.HIGHEST


def _params(*sem):
    return pltpu.CompilerParams(dimension_semantics=sem, vmem_limit_bytes=VMEM_LIMIT)


def _dot(a, b, precision=None):
    return lax.dot_general(a, b, (((1,), (0,)), ((), ())), precision=precision,
                           preferred_element_type=F32)


def _dot_nt(a, b, precision=None):
    return lax.dot_general(a, b, (((1,), (1,)), ((), ())), precision=precision,
                           preferred_element_type=F32)


def _dot_tn(a, b, precision=None):
    return lax.dot_general(a, b, (((0,), (0,)), ((), ())), precision=precision,
                           preferred_element_type=F32)


def _rms(x):
    return x * lax.rsqrt(jnp.mean(x * x, axis=-1, keepdims=True) + RMS_EPS)


def _sigmoid(x):
    return 1.0 / (1.0 + jnp.exp(-x))


def _softplus(x):
    return jnp.maximum(x, 0.0) + jnp.log(1.0 + jnp.exp(-jnp.abs(x)))


def _full(shape):
    nd = len(shape)
    return pl.BlockSpec(shape, lambda *_: (0,) * nd)


def _rwkv_pre_kernel(h_ref, hp_ref, gn_ref, mix_ref, wrkv_ref, w0_ref, w1_ref, w2_ref,
                     a0_ref, a1_ref, a2_ref, g1_ref, g2_ref, kk_ref, ka_ref,
                     r_out, k_out, v_out, lw_out, kk_out, a_out, g_out, *, tiles_per_seq):
    i = pl.program_id(0)
    gn = gn_ref[...]
    hn = _rms(h_ref[...]) * gn
    hpn = _rms(hp_ref[7:8, :]) * gn
    hpn = jnp.where(i % tiles_per_seq == 0, 0.0, hpn)
    row = lax.broadcasted_iota(jnp.int32, hn.shape, 0)
    prev = jnp.where(row == 0, hpn, pltpu.roll(hn, 1, axis=0))
    xx = prev - hn

    def mixed(n):
        return (hn + xx * mix_ref[n:n + 1, :]).astype(BF16)

    r = _dot(mixed(0), wrkv_ref[0])
    k = _dot(mixed(1), wrkv_ref[1])
    v = _dot(mixed(2), wrkv_ref[2])
    wl = w0_ref[...] + _dot(jnp.tanh(_dot(mixed(3), w1_ref[...])).astype(BF16), w2_ref[...])
    a = _sigmoid(a0_ref[...] + _dot(_dot(mixed(4), a1_ref[...]).astype(BF16), a2_ref[...]))
    g = _dot(_sigmoid(_dot(mixed(5), g1_ref[...])).astype(BF16), g2_ref[...])
    r_out[...] = r
    k_out[...] = k * (1.0 + (a - 1.0) * ka_ref[...])
    v_out[...] = v
    lw_out[...] = -jnp.exp(-_softplus(-wl) - 0.5)
    kk_out[...] = k * kk_ref[...]
    a_out[...] = a
    g_out[...] = g


def _rwkv_pre(h, seq, gn, mix, wrkv, w0, w1, w2, a0, a1, a2, g1, g2, k_k, k_a, tm=256):
    m, c = h.shape
    assert seq % tm == 0 and m % seq == 0
    row = lambda i: (i, 0)
    out = jax.ShapeDtypeStruct((m, c), F32)
    ins = [gn, mix, wrkv, w0, w1, w2, a0, a1, a2, g1, g2, k_k, k_a]
    return pl.pallas_call(
        functools.partial(_rwkv_pre_kernel, tiles_per_seq=seq // tm),
        out_shape=[out] * 7,
        grid=(m // tm,),
        in_specs=[pl.BlockSpec((tm, c), row),
                  pl.BlockSpec((8, c), lambda i: (jnp.maximum(i * (tm // 8) - 1, 0), 0))]
                 + [_full(t.shape) for t in ins],
        out_specs=[pl.BlockSpec((tm, c), row)] * 7,
        compiler_params=_params("parallel"),
        name="rwkv_pre",
    )(h, h, *ins)


def _wkv_kernel(r_ref, lw_ref, k_ref, v_ref, kk_ref, a_ref, rk_ref, gw_ref, gb_ref,
                y_ref, st_ref, *, chunks, prec):
    L = WKV_CHUNK

    @pl.when(pl.program_id(2) == 0)
    def _():
        st_ref[...] = jnp.zeros_like(st_ref)

    lane = lax.broadcasted_iota(jnp.int32, (1, LANES), 1)
    head0 = lane < HEAD_DIM
    rr = lax.broadcasted_iota(jnp.int32, (LANES, LANES), 0)
    cc = lax.broadcasted_iota(jnp.int32, (LANES, LANES), 1)
    same_head = (rr < HEAD_DIM) == (cc < HEAD_DIM)
    bd = same_head.astype(F32)
    eye = rr == cc
    tr = lax.broadcasted_iota(jnp.int32, (L, L), 0)
    tc = lax.broadcasted_iota(jnp.int32, (L, L), 1)
    strict = tr > tc
    incl = tr >= tc
    tri = incl.astype(F32)
    eye_l = (tr == tc).astype(F32)
    m0 = head0.astype(F32)
    m1 = 1.0 - m0
    sel = lambda x0, x1: jnp.where(head0, x0, x1)

    for ci in range(chunks):
        sl = pl.ds(ci * L, L)
        r = r_ref[0, sl, :]
        lw = lw_ref[0, sl, :]
        k = k_ref[0, sl, :]
        v = v_ref[0, sl, :]
        kkr = kk_ref[0, sl, :]
        ag = a_ref[0, sl, :]

        cum = _dot(tri, lw, HIGHEST)
        cum_l = cum[L - 1:L, :]
        w_in = jnp.exp(cum)
        w_ex = jnp.exp(cum - lw)
        w_inv = jnp.exp(-cum)
        w_tail = jnp.exp(cum_l - cum)
        kk = kkr / jnp.maximum(jnp.sqrt(_dot(kkr * kkr, bd, HIGHEST)), 1e-12)
        b = kk * ag
        at = -kk * w_ex
        bt = b * w_inv
        kt = k * w_inv
        rt = r * w_in
        bh = b * w_tail
        kh = k * w_tail

        lhs = jnp.concatenate([at * m0, at * m1, rt * m0, rt * m1], axis=0)
        pb = _dot_nt(lhs, bt, prec)
        pk = _dot_nt(lhs, kt, prec)
        ta, u, rp, yi = [], [], [], []
        for h in range(HEADS_PER_BLOCK):
            a_ab = jnp.where(strict, pb[h * L:(h + 1) * L], 0.0)
            a_ak = jnp.where(strict, pk[h * L:(h + 1) * L], 0.0)
            a_rb = jnp.where(incl, pb[(2 + h) * L:(3 + h) * L], 0.0)
            a_rk = jnp.where(incl, pk[(2 + h) * L:(3 + h) * L], 0.0)
            t = eye_l + a_ab
            p = a_ab
            for _ in range(5):
                p = _dot(p, p, prec)
                t = t + _dot(t, p, prec)
            ta_h = _dot(t, at, prec)
            u_h = _dot(t, _dot(a_ak, v, prec), prec)
            ta.append(ta_h)
            u.append(u_h)
            rp.append(_dot(a_rb, ta_h, prec))
            yi.append(_dot(a_rb, u_h, prec) + _dot(a_rk, v, prec))
        ta = sel(*ta)
        u = sel(*u)
        rp = rt + sel(*rp)
        yi = sel(*yi)

        st = st_ref[...]
        y = _dot(rp, st, prec) + yi
        mt = jnp.where(eye, jnp.exp(cum_l), 0.0) + bd * _dot_tn(bh, ta, prec)
        gt = bd * (_dot_tn(bh, u, prec) + _dot_tn(kh, v, prec))
        st_ref[...] = _dot(mt, st, prec) + gt

        mu = _dot(y, bd, HIGHEST) * (1.0 / HEAD_DIM)
        d = y - mu
        var = _dot(d * d, bd, HIGHEST) * (1.0 / HEAD_DIM)
        yn = d * lax.rsqrt(var + GN_EPS) * gw_ref[...] + gb_ref[...]
        bonus = _dot(r * k * rk_ref[...], bd, HIGHEST) * v
        y_ref[0, sl, :] = yn + bonus


def _wkv(r, lw, k, v, kk, a, r_k, lnx_w, lnx_b, prec, chunks=4):
    bsz, seq, c = r.shape
    tl = chunks * WKV_CHUNK
    assert seq % tl == 0 and c % LANES == 0
    tok = pl.BlockSpec((1, tl, LANES), lambda b, h, t: (b, t, h))
    par = pl.BlockSpec((1, LANES), lambda b, h, t: (0, h))
    return pl.pallas_call(
        functools.partial(_wkv_kernel, chunks=chunks, prec=prec),
        out_shape=jax.ShapeDtypeStruct((bsz, seq, c), F32),
        grid=(bsz, c // LANES, seq // tl),
        in_specs=[tok] * 6 + [par] * 3,
        out_specs=tok,
        scratch_shapes=[pltpu.VMEM((LANES, LANES), F32)],
        compiler_params=_params("parallel", "parallel", "arbitrary"),
        name="wkv",
    )(r, lw, k, v, kk, a, r_k, lnx_w, lnx_b)


def _proj_res_kernel(*refs, gated):
    if gated:
        x_ref, g_ref, w_ref, res_ref, o_ref = refs
        x = (x_ref[...] * g_ref[...]).astype(BF16)
    else:
        x_ref, w_ref, res_ref, o_ref = refs
        x = x_ref[...]
    o_ref[...] = res_ref[...] + _dot(x, w_ref[...])


def _proj_res(x, gate, w, res, tm=512):
    m, c = res.shape
    row = pl.BlockSpec((tm, c), lambda i: (i, 0))
    gated = gate is not None
    ins = [x, gate, w, res] if gated else [x, w, res]
    specs = [row, row, _full(w.shape), row] if gated else [row, _full(w.shape), row]
    return pl.pallas_call(
        functools.partial(_proj_res_kernel, gated=gated),
        out_shape=jax.ShapeDtypeStruct((m, c), F32),
        grid=(m // tm,),
        in_specs=specs,
        out_specs=row,
        compiler_params=_params("parallel"),
        name="proj_res",
    )(*ins)


def _moe_kernel(h_ref, gn_ref, wr_ref, br_ref, wg_ref, wu_ref, wd_ref, gf_ref, o_ref,
                xn_ref, gates_ref, acc_ref, *, final_norm):
    e = pl.program_id(1)
    lane = lax.broadcasted_iota(jnp.int32, (1, LANES), 1)

    @pl.when(e == 0)
    def _():
        xn = _rms(h_ref[...]) * gn_ref[...]
        xn_ref[...] = xn.astype(BF16)
        logits = _dot(xn, wr_ref[...], HIGHEST) + br_ref[...]
        neg = jnp.float32(-jnp.inf)
        is_g = lane < N_GROUPS
        gl = jnp.where(is_g, logits, neg)
        gmax = jnp.max(gl, axis=-1, keepdims=True)
        gval = 1.0 / jnp.sum(jnp.exp(gl - gmax), axis=-1, keepdims=True)
        gidx = jnp.min(jnp.where(gl == gmax, lane, LANES), axis=-1, keepdims=True)
        ex = lane - N_GROUPS
        in_grp = (ex >= gidx * EXPERTS_PER_GROUP) & (ex < (gidx + 1) * EXPERTS_PER_GROUP)
        il = jnp.where(in_grp, logits, neg)
        l1 = jnp.max(il, axis=-1, keepdims=True)
        i1 = jnp.min(jnp.where(il == l1, lane, LANES), axis=-1, keepdims=True)
        il2 = jnp.where(lane == i1, neg, il)
        l2 = jnp.max(il2, axis=-1, keepdims=True)
        i2 = jnp.min(jnp.where(il2 == l2, lane, LANES), axis=-1, keepdims=True)
        e2 = jnp.exp(l2 - l1)
        w1 = gval / (1.0 + e2)
        w2 = gval * e2 / (1.0 + e2)
        gates_ref[...] = jnp.where(lane == i1, w1, 0.0) + jnp.where(lane == i2, w2, 0.0)
        acc_ref[...] = jnp.zeros_like(acc_ref)

    xn = xn_ref[...]
    gate_e = jnp.sum(jnp.where(lane == e + N_GROUPS, gates_ref[...], 0.0), axis=-1, keepdims=True)
    hg = _dot(xn, wg_ref[0])
    hu = _dot(xn, wu_ref[0])
    hm = (hg * _sigmoid(hg)) * hu * gate_e
    acc_ref[...] += _dot(hm.astype(BF16), wd_ref[0])

    @pl.when(e == N_EXPERTS - 1)
    def _():
        out = h_ref[...] + acc_ref[...]
        if final_norm:
            out = _rms(out) * gf_ref[...]
        o_ref[...] = out


def _moe(h, gn, w_router, b_router, w_gate, w_up, w_down, g_final, final_norm, tm=512):
    m, c = h.shape
    f = w_gate.shape[-1]
    row = pl.BlockSpec((tm, c), lambda i, e: (i, 0))
    return pl.pallas_call(
        functools.partial(_moe_kernel, final_norm=final_norm),
        out_shape=jax.ShapeDtypeStruct((m, c), F32),
        grid=(m // tm, N_EXPERTS),
        in_specs=[row, _full(gn.shape), _full(w_router.shape), _full(b_router.shape),
                  pl.BlockSpec((1, c, f), lambda i, e: (e, 0, 0)),
                  pl.BlockSpec((1, c, f), lambda i, e: (e, 0, 0)),
                  pl.BlockSpec((1, f, c), lambda i, e: (e, 0, 0)),
                  _full(g_final.shape)],
        out_specs=row,
        scratch_shapes=[pltpu.VMEM((tm, c), BF16), pltpu.VMEM((tm, LANES), F32),
                        pltpu.VMEM((tm, c), F32)],
        compiler_params=_params("parallel", "arbitrary"),
        name="moe",
    )(h, gn, w_router, b_router, w_gate, w_up, w_down, g_final)


def _norm_proj_kernel(h_ref, gn_ref, w_ref, o_ref, *, scale):
    xn = (_rms(h_ref[...]) * gn_ref[...]).astype(BF16)
    o_ref[...] = (_dot(xn, w_ref[...]) * scale).astype(o_ref.dtype)


def _norm_proj(h, gn, w, scale, tm=512):
    m, c = h.shape
    n = w.shape[1]
    return pl.pallas_call(
        functools.partial(_norm_proj_kernel, scale=scale),
        out_shape=jax.ShapeDtypeStruct((m, n), BF16),
        grid=(m // tm,),
        in_specs=[pl.BlockSpec((tm, c), lambda i: (i, 0)), _full(gn.shape), _full(w.shape)],
        out_specs=pl.BlockSpec((tm, n), lambda i: (i, 0)),
        compiler_params=_params("parallel"),
        name="norm_proj",
    )(h, gn, w)


def _sb_attn_kernel(q_ref, k_ref, v_ref, o_ref, *, tq):
    i = pl.program_id(2)
    sub = LANES
    nsub = tq // sub
    lane = lax.broadcasted_iota(jnp.int32, (1, LANES), 1)
    head0 = lane < HEAD_DIM
    jr = lax.broadcasted_iota(jnp.int32, (sub, 2 * sub), 0)
    sc = lax.broadcasted_iota(jnp.int32, (sub, 2 * sub), 1)
    suffix_mat = ((jr > sc) | (sc >= sub)).astype(BF16)
    ti = lax.broadcasted_iota(jnp.int32, (tq, sub), 0)
    sj = lax.broadcasted_iota(jnp.int32, (tq, sub), 1)

    q = q_ref[0]
    zero = jnp.zeros((), q.dtype)
    qh = [jnp.where(head0, q, zero), jnp.where(head0, zero, q)]

    def block(base, state, masked):
        out = []
        for h in range(HEADS_PER_BLOCK):
            carry, acc = state[h]
            for sb in reversed(range(nsub)):
                off = pl.multiple_of(base + sb * sub, sub)
                kb = k_ref[0, pl.ds(off, sub), :]
                vb = v_ref[0, pl.ds(off, sub), :]
                z = _dot_nt(qh[h], kb)
                sp = _softplus(z)
                if masked:
                    keep = sj + sb * sub < ti
                    sp = jnp.where(keep, sp, 0.0)
                res = _dot(sp.astype(BF16), suffix_mat)
                att = jnp.exp(z - sp - res[:, :sub] - carry)
                if masked:
                    att = jnp.where(keep, att, 0.0)
                acc = acc + _dot(att.astype(BF16), vb)
                carry = carry + res[:, sub:]
            out.append((carry, acc))
        return tuple(out)

    zeros = jnp.zeros((tq, LANES), F32)
    state = ((zeros, zeros), (zeros, zeros))
    state = block(i * tq, state, True)
    state = lax.fori_loop(0, i, lambda j, s: block((i - 1 - j) * tq, s, False), state)
    o_ref[0] = jnp.where(head0, state[0][1], state[1][1]).astype(o_ref.dtype)


def _sb_attn(q, kv, tq=256):
    bsz, seq, c = q.shape
    assert seq % tq == 0
    nhb = c // LANES
    kspec = pl.BlockSpec((1, seq, LANES), lambda b, h, i: (b, 0, h))
    vspec = pl.BlockSpec((1, seq, LANES), lambda b, h, i: (b, 0, nhb + h))
    qo = pl.BlockSpec((1, tq, LANES), lambda b, h, i: (b, i, h))
    return pl.pallas_call(
        functools.partial(_sb_attn_kernel, tq=tq),
        out_shape=jax.ShapeDtypeStruct((bsz, seq, c), BF16),
        grid=(bsz, c // LANES, seq // tq),
        in_specs=[qo, kspec, vspec],
        out_specs=qo,
        compiler_params=_params("parallel", "parallel", "arbitrary"),
        name="sb_attn",
    )(q, kv, kv)


def _router_params(w_group, b_group, w_inner, b_inner):
    c = w_group.shape[0]
    w = jnp.concatenate([w_group, w_inner.reshape(c, N_EXPERTS)], axis=1)
    b = jnp.concatenate([b_group, b_inner.reshape(N_EXPERTS)])
    pad = LANES - w.shape[1]
    return jnp.pad(w, ((0, 0), (0, pad))), jnp.pad(b, (0, pad)).reshape(1, LANES)


def kernel(x, norm_mix, norm_ffn, norm_kv, norm_final, rw_mix, rw_w_rkv, rw_w0, rw_w1, rw_w2, rw_a0, rw_a1, rw_a2, rw_g1, rw_g2, rw_k_k, rw_k_a, rw_r_k, rw_lnx_w, rw_lnx_b, rw_w_out, w_kv, sb_w_q, sb_w_out, moe_w_group, moe_b_group, moe_w_inner, moe_b_inner, moe_w_gate, moe_w_up, moe_w_down):
    bsz, seq, c = x.shape
    depth = norm_mix.shape[0]
    n_a = rw_mix.shape[0]
    m = bsz * seq
    vec = lambda t: t.reshape(1, c)
    bf = lambda t: t.astype(BF16)

    h = x.reshape(m, c)
    kv_sh = None
    for l in range(depth):
        if l < n_a:
            i = l
            r, k, v, lw, kk, a, g = _rwkv_pre(
                h, seq, vec(norm_mix[l]), jnp.pad(rw_mix[i], ((0, 2), (0, 0))), bf(rw_w_rkv[i]),
                vec(rw_w0[i]), bf(rw_w1[i]), bf(rw_w2[i]), vec(rw_a0[i]), bf(rw_a1[i]), bf(rw_a2[i]),
                bf(rw_g1[i]), bf(rw_g2[i]), vec(rw_k_k[i]), vec(rw_k_a[i]))
            s3 = lambda t: t.reshape(bsz, seq, c)
            y = _wkv(s3(r), s3(lw), s3(k), s3(v), s3(kk), s3(a), vec(rw_r_k[i]),
                     vec(rw_lnx_w[i]), vec(rw_lnx_b[i]), HIGHEST)
            h = _proj_res(y.reshape(m, c), g, bf(rw_w_out[i]), h)
        else:
            j = l - n_a
            q = _norm_proj(h, vec(norm_mix[l]), bf(sb_w_q[j]), HEAD_DIM ** -0.5)
            o = _sb_attn(q.reshape(bsz, seq, c), kv_sh)
            h = _proj_res(o.reshape(m, c), None, bf(sb_w_out[j]), h)
        w_r, b_r = _router_params(moe_w_group[l], moe_b_group[l], moe_w_inner[l], moe_b_inner[l])
        h = _moe(h, vec(norm_ffn[l]), w_r, b_r, bf(moe_w_gate[l]), bf(moe_w_up[l]),
                 bf(moe_w_down[l]), vec(norm_final), l == depth - 1)
        if l == n_a - 1:
            kv_sh = _norm_proj(h, vec(norm_kv), bf(w_kv), 1.0).reshape(bsz, seq, 2 * c)
    return h.reshape(bsz, seq, c)
```

```python
import functools

import jax
import jax.numpy as jnp
from jax import lax
from jax.experimental import pallas as pl
from jax.experimental.pallas import tpu as pltpu

F32 = jnp.float32
BF16 = jnp.bfloat16

LANES = 128
HEAD_DIM = 64
HEADS_PER_BLOCK = LANES // HEAD_DIM
RMS_EPS = 1e-6
GN_EPS = 64e-5
N_GROUPS = 4
EXPERTS_PER_GROUP = 4
N_EXPERTS = N_GROUPS * EXPERTS_PER_GROUP
WKV_CHUNK = 64
VMEM_LIMIT = 56 * 1024 * 1024

HIGHEST = lax.Precision.HIGHEST
LOG2E = 1.4426950408889634


def _params(*sem):
    return pltpu.CompilerParams(dimension_semantics=sem, vmem_limit_bytes=VMEM_LIMIT)


def _dot(a, b, precision=None):
    return lax.dot_general(a, b, (((1,), (0,)), ((), ())), precision=precision,
                           preferred_element_type=F32)


def _dot_nt(a, b, precision=None):
    return lax.dot_general(a, b, (((1,), (1,)), ((), ())), precision=precision,
                           preferred_element_type=F32)


def _dot_tn(a, b, precision=None):
    return lax.dot_general(a, b, (((0,), (0,)), ((), ())), precision=precision,
                           preferred_element_type=F32)


def _bf16_terms(x, n):
    terms = []
    for _ in range(n):
        t = x.astype(BF16)
        terms.append(t)
        x = x - t.astype(F32)
    return terms


def _dot_mask_rhs(x, mask01, n):
    m = mask01.astype(BF16)
    return sum(_dot(t, m) for t in _bf16_terms(x, n))


def _dot_mask_lhs(mask01, x, n):
    m = mask01.astype(BF16)
    return sum(_dot(m, t) for t in _bf16_terms(x, n))


def _rms(x):
    return x * lax.rsqrt(jnp.mean(x * x, axis=-1, keepdims=True) + RMS_EPS)


def _sigmoid(x):
    return 1.0 / (1.0 + jnp.exp(-x))


def _softplus(x):
    return jnp.maximum(x, 0.0) + jnp.log(1.0 + jnp.exp(-jnp.abs(x)))


def _full(shape):
    nd = len(shape)
    return pl.BlockSpec(shape, lambda *_: (0,) * nd)


def _rwkv_pre_kernel(h_ref, hp_ref, gn_ref, mix_ref, wrkv_ref, w0_ref, w1_ref, w2_ref,
                     a0_ref, a1_ref, a2_ref, g1_ref, g2_ref, kk_ref, ka_ref,
                     r_out, k_out, v_out, lw_out, kk_out, a_out, g_out, *, tiles_per_seq):
    i = pl.program_id(0)
    gn = gn_ref[...]
    hn = _rms(h_ref[...]) * gn
    hpn = _rms(hp_ref[7:8, :]) * gn
    hpn = jnp.where(i % tiles_per_seq == 0, 0.0, hpn)
    row = lax.broadcasted_iota(jnp.int32, hn.shape, 0)
    prev = jnp.where(row == 0, hpn, pltpu.roll(hn, 1, axis=0))
    xx = prev - hn

    def mixed(n):
        return (hn + xx * mix_ref[n:n + 1, :]).astype(BF16)

    r = _dot(mixed(0), wrkv_ref[0])
    k = _dot(mixed(1), wrkv_ref[1])
    v = _dot(mixed(2), wrkv_ref[2])
    wl = w0_ref[...] + _dot(jnp.tanh(_dot(mixed(3), w1_ref[...])).astype(BF16), w2_ref[...])
    a = _sigmoid(a0_ref[...] + _dot(_dot(mixed(4), a1_ref[...]).astype(BF16), a2_ref[...]))
    g = _dot(_sigmoid(_dot(mixed(5), g1_ref[...])).astype(BF16), g2_ref[...])
    r_out[...] = r
    k_out[...] = k * (1.0 + (a - 1.0) * ka_ref[...])
    v_out[...] = v
    lw_out[...] = -jnp.exp(-_softplus(-wl) - 0.5)
    kk_out[...] = k * kk_ref[...]
    a_out[...] = a
    g_out[...] = g


def _rwkv_pre(h, seq, gn, mix, wrkv, w0, w1, w2, a0, a1, a2, g1, g2, k_k, k_a, tm=256):
    m, c = h.shape
    assert seq % tm == 0 and m % seq == 0
    row = lambda i: (i, 0)
    out = jax.ShapeDtypeStruct((m, c), F32)
    ins = [gn, mix, wrkv, w0, w1, w2, a0, a1, a2, g1, g2, k_k, k_a]
    return pl.pallas_call(
        functools.partial(_rwkv_pre_kernel, tiles_per_seq=seq // tm),
        out_shape=[out] * 7,
        grid=(m // tm,),
        in_specs=[pl.BlockSpec((tm, c), row),
                  pl.BlockSpec((8, c), lambda i: (jnp.maximum(i * (tm // 8) - 1, 0), 0))]
                 + [_full(t.shape) for t in ins],
        out_specs=[pl.BlockSpec((tm, c), row)] * 7,
        compiler_params=_params("parallel"),
        name="rwkv_pre",
    )(h, h, *ins)


def _wkv_kernel(r_ref, lw_ref, k_ref, v_ref, kk_ref, a_ref, rk_ref, gw_ref, gb_ref,
                y_ref, st_ref, *, chunks):
    L = WKV_CHUNK

    @pl.when(pl.program_id(2) == 0)
    def _():
        st_ref[...] = jnp.zeros_like(st_ref)

    tl = chunks * L
    lane = lax.broadcasted_iota(jnp.int32, (1, LANES), 1)
    head0 = lane < HEAD_DIM
    rr = lax.broadcasted_iota(jnp.int32, (LANES, LANES), 0)
    cc = lax.broadcasted_iota(jnp.int32, (LANES, LANES), 1)
    bd = ((rr < HEAD_DIM) == (cc < HEAD_DIM)).astype(F32)
    eye = rr == cc
    tr = lax.broadcasted_iota(jnp.int32, (tl, tl), 0)
    tc = lax.broadcasted_iota(jnp.int32, (tl, tl), 1)
    shift = L.bit_length() - 1
    same_chunk = (tr >> shift) == (tc >> shift)
    strict = same_chunk & (tr > tc)
    incl = same_chunk & (tr >= tc)
    tri = incl.astype(F32)
    eye_t = (tr == tc).astype(F32)
    m0 = head0.astype(F32)
    masks = (m0, 1.0 - m0)
    head0w = jnp.concatenate([head0, head0], axis=1)
    sel = lambda x0, x1: jnp.where(head0 if x0.shape[1] == LANES else head0w, x0, x1)

    r = r_ref[0]
    lw = lw_ref[0]
    k = k_ref[0]
    v = v_ref[0]
    kkr = kk_ref[0]
    ag = a_ref[0]

    cum = _dot_mask_lhs(tri, lw, 3)
    cum_end = [cum[(c + 1) * L - 1:(c + 1) * L, :] for c in range(chunks)]
    cum_l = jnp.concatenate([jnp.broadcast_to(e, (L, LANES)) for e in cum_end], axis=0)
    w_in = jnp.exp(cum)
    w_ex = jnp.exp(cum - lw)
    w_inv = jnp.exp(-cum)
    w_tail = jnp.exp(cum_l - cum)
    kk = kkr / jnp.maximum(jnp.sqrt(_dot_mask_rhs(kkr * kkr, bd, 1)), 1e-12)
    b = kk * ag
    at = -kk * w_ex
    at16 = at.astype(BF16)
    bt16 = (b * w_inv).astype(BF16)
    kt16 = (k * w_inv).astype(BF16)
    rt = r * w_in
    bh16 = (b * w_tail).astype(BF16)
    kh16 = (k * w_tail).astype(BF16)
    v16 = v.astype(BF16)

    heads = range(HEADS_PER_BLOCK)
    a_ab, a_ak, a_rb, a_rk = [], [], [], []
    for h in heads:
        lhs = jnp.concatenate([at * masks[h], rt * masks[h]], axis=0).astype(BF16)
        pb = _dot_nt(lhs, bt16)
        pk = _dot_nt(lhs, kt16)
        a_ab.append(jnp.where(strict, pb[:tl], 0.0))
        a_ak.append(jnp.where(strict, pk[:tl], 0.0).astype(BF16))
        a_rb.append(jnp.where(incl, pb[tl:], 0.0).astype(BF16))
        a_rk.append(jnp.where(incl, pk[tl:], 0.0).astype(BF16))
    t = [eye_t + a for a in a_ab]
    p = [a.astype(BF16) for a in a_ab]
    akv = [_dot(a_ak[h], v16).astype(BF16) for h in heads]
    rkv = [_dot(a_rk[h], v16) for h in heads]
    for _ in range(5):
        p = [_dot(p[h], p[h]).astype(BF16) for h in heads]
        t = [t[h] + _dot(t[h].astype(BF16), p[h]) for h in heads]
    tau = [_dot(t[h].astype(BF16), jnp.concatenate([at16, akv[h]], axis=1)).astype(BF16) for h in heads]
    rpy = [_dot(a_rb[h], tau[h]) for h in heads]
    tau16 = sel(*tau)
    ta16 = tau16[:, :LANES]
    u16 = tau16[:, LANES:]
    rpy = sel(*rpy)
    rp16 = (rt + rpy[:, :LANES]).astype(BF16)
    yi = rpy[:, LANES:] + sel(*rkv)

    mts, gts = [], []
    for c in range(chunks):
        rows = slice(c * L, (c + 1) * L)
        mt = jnp.where(eye, jnp.exp(cum_end[c]), 0.0) + bd * _dot_tn(bh16[rows], ta16[rows])
        mts.append(mt.astype(BF16))
        gts.append(bd * (_dot_tn(bh16[rows], u16[rows]) + _dot_tn(kh16[rows], v16[rows])))
    st = st_ref[...]
    ys = []
    for c in range(chunks):
        rows = slice(c * L, (c + 1) * L)
        st16 = st.astype(BF16)
        ys.append(_dot(rp16[rows], st16) + yi[rows])
        st = _dot(mts[c], st16) + gts[c]
    st_ref[...] = st
    y = jnp.concatenate(ys, axis=0)

    mu = _dot_mask_rhs(y, bd, 1) * (1.0 / HEAD_DIM)
    d = y - mu
    var = _dot_mask_rhs(d * d, bd, 1) * (1.0 / HEAD_DIM)
    yn = d * lax.rsqrt(var + GN_EPS) * gw_ref[...] + gb_ref[...]
    bonus = _dot_mask_rhs(r * k * rk_ref[...], bd, 1) * v
    y_ref[0] = yn + bonus


def _wkv(r, lw, k, v, kk, a, r_k, lnx_w, lnx_b, chunks=4):
    bsz, seq, c = r.shape
    tl = chunks * WKV_CHUNK
    assert seq % tl == 0 and c % LANES == 0
    tok = pl.BlockSpec((1, tl, LANES), lambda b, h, t: (b, t, h))
    par = pl.BlockSpec((1, LANES), lambda b, h, t: (0, h))
    return pl.pallas_call(
        functools.partial(_wkv_kernel, chunks=chunks),
        out_shape=jax.ShapeDtypeStruct((bsz, seq, c), F32),
        grid=(bsz, c // LANES, seq // tl),
        in_specs=[tok] * 6 + [par] * 3,
        out_specs=tok,
        scratch_shapes=[pltpu.VMEM((LANES, LANES), F32)],
        compiler_params=_params("parallel", "parallel", "arbitrary"),
        name="wkv",
    )(r, lw, k, v, kk, a, r_k, lnx_w, lnx_b)


def _proj_res_kernel(*refs, gated):
    if gated:
        x_ref, g_ref, w_ref, res_ref, o_ref = refs
        x = (x_ref[...] * g_ref[...]).astype(BF16)
    else:
        x_ref, w_ref, res_ref, o_ref = refs
        x = x_ref[...]
    o_ref[...] = res_ref[...] + _dot(x, w_ref[...])


def _proj_res(x, gate, w, res, tm=512):
    m, c = res.shape
    row = pl.BlockSpec((tm, c), lambda i: (i, 0))
    gated = gate is not None
    ins = [x, gate, w, res] if gated else [x, w, res]
    specs = [row, row, _full(w.shape), row] if gated else [row, _full(w.shape), row]
    return pl.pallas_call(
        functools.partial(_proj_res_kernel, gated=gated),
        out_shape=jax.ShapeDtypeStruct((m, c), F32),
        grid=(m // tm,),
        in_specs=specs,
        out_specs=row,
        compiler_params=_params("parallel"),
        name="proj_res",
    )(*ins)


def _moe_kernel(h_ref, gn_ref, wr_ref, br_ref, wg_ref, wu_ref, wd_ref, gf_ref, o_ref,
                xn_ref, gates_ref, acc_ref, *, final_norm):
    e = pl.program_id(1)
    lane = lax.broadcasted_iota(jnp.int32, (1, LANES), 1)

    @pl.when(e == 0)
    def _():
        xn = _rms(h_ref[...]) * gn_ref[...]
        xn_ref[...] = xn.astype(BF16)
        logits = _dot(xn, wr_ref[...], HIGHEST) + br_ref[...]
        neg = jnp.float32(-jnp.inf)
        is_g = lane < N_GROUPS
        gl = jnp.where(is_g, logits, neg)
        gmax = jnp.max(gl, axis=-1, keepdims=True)
        gval = 1.0 / jnp.sum(jnp.exp(gl - gmax), axis=-1, keepdims=True)
        gidx = jnp.min(jnp.where(gl == gmax, lane, LANES), axis=-1, keepdims=True)
        ex = lane - N_GROUPS
        in_grp = (ex >= gidx * EXPERTS_PER_GROUP) & (ex < (gidx + 1) * EXPERTS_PER_GROUP)
        il = jnp.where(in_grp, logits, neg)
        l1 = jnp.max(il, axis=-1, keepdims=True)
        i1 = jnp.min(jnp.where(il == l1, lane, LANES), axis=-1, keepdims=True)
        il2 = jnp.where(lane == i1, neg, il)
        l2 = jnp.max(il2, axis=-1, keepdims=True)
        i2 = jnp.min(jnp.where(il2 == l2, lane, LANES), axis=-1, keepdims=True)
        e2 = jnp.exp(l2 - l1)
        w1 = gval / (1.0 + e2)
        w2 = gval * e2 / (1.0 + e2)
        gates_ref[...] = jnp.where(lane == i1, w1, 0.0) + jnp.where(lane == i2, w2, 0.0)
        acc_ref[...] = jnp.zeros_like(acc_ref)

    xn = xn_ref[...]
    gate_e = jnp.sum(jnp.where(lane == e + N_GROUPS, gates_ref[...], 0.0), axis=-1, keepdims=True)
    hg = _dot(xn, wg_ref[0])
    hu = _dot(xn, wu_ref[0])
    hm = (hg * _sigmoid(hg)) * hu * gate_e
    acc_ref[...] += _dot(hm.astype(BF16), wd_ref[0])

    @pl.when(e == N_EXPERTS - 1)
    def _():
        out = h_ref[...] + acc_ref[...]
        if final_norm:
            out = _rms(out) * gf_ref[...]
        o_ref[...] = out


def _moe(h, gn, w_router, b_router, w_gate, w_up, w_down, g_final, final_norm, tm=512):
    m, c = h.shape
    f = w_gate.shape[-1]
    row = pl.BlockSpec((tm, c), lambda i, e: (i, 0))
    return pl.pallas_call(
        functools.partial(_moe_kernel, final_norm=final_norm),
        out_shape=jax.ShapeDtypeStruct((m, c), F32),
        grid=(m // tm, N_EXPERTS),
        in_specs=[row, _full(gn.shape), _full(w_router.shape), _full(b_router.shape),
                  pl.BlockSpec((1, c, f), lambda i, e: (e, 0, 0)),
                  pl.BlockSpec((1, c, f), lambda i, e: (e, 0, 0)),
                  pl.BlockSpec((1, f, c), lambda i, e: (e, 0, 0)),
                  _full(g_final.shape)],
        out_specs=row,
        scratch_shapes=[pltpu.VMEM((tm, c), BF16), pltpu.VMEM((tm, LANES), F32),
                        pltpu.VMEM((tm, c), F32)],
        compiler_params=_params("parallel", "arbitrary"),
        name="moe",
    )(h, gn, w_router, b_router, w_gate, w_up, w_down, g_final)


def _norm_proj_kernel(h_ref, gn_ref, w_ref, o_ref, *, scale):
    xn = (_rms(h_ref[...]) * gn_ref[...]).astype(BF16)
    o_ref[...] = (_dot(xn, w_ref[...]) * scale).astype(o_ref.dtype)


def _norm_proj(h, gn, w, scale, tm=512):
    m, c = h.shape
    n = w.shape[1]
    return pl.pallas_call(
        functools.partial(_norm_proj_kernel, scale=scale),
        out_shape=jax.ShapeDtypeStruct((m, n), BF16),
        grid=(m // tm,),
        in_specs=[pl.BlockSpec((tm, c), lambda i: (i, 0)), _full(gn.shape), _full(w.shape)],
        out_specs=pl.BlockSpec((tm, n), lambda i: (i, 0)),
        compiler_params=_params("parallel"),
        name="norm_proj",
    )(h, gn, w)


def _sb_attn_kernel(q_ref, k_ref, v_ref, o_ref, carry_ref, acc_ref, sp_ref, zs_ref, tot_ref, *, tq):
    i = pl.program_id(2)
    tk = tq
    lane = lax.broadcasted_iota(jnp.int32, (1, LANES), 1)
    head0 = lane < HEAD_DIM
    jr = lax.broadcasted_iota(jnp.int32, (tk, tk), 0)
    sc = lax.broadcasted_iota(jnp.int32, (tk, tk), 1)
    suffix_mat = (jr > sc).astype(BF16)
    jt = lax.broadcasted_iota(jnp.int32, (2 * LANES, 2 * LANES), 0)
    st = lax.broadcasted_iota(jnp.int32, (2 * LANES, 2 * LANES), 1)
    total_mat = ((jt < LANES) == (st < LANES)).astype(BF16)

    q = q_ref[0]
    zero = jnp.zeros((), q.dtype)
    qh = [jnp.where(head0, q, zero), jnp.where(head0, zero, q)]
    sign = jnp.uint32(0x80000000)

    heads = range(HEADS_PER_BLOCK)
    keep = sc < jr

    def logits(base, masked):
        kb = k_ref[0, pl.ds(pl.multiple_of(base, tk), tk), :]
        sp16, zs, halves = [], [], []
        for h in heads:
            z = _dot_nt(qh[h], kb)
            nabs = lax.bitcast_convert_type(lax.bitcast_convert_type(z, jnp.uint32) | sign, F32)
            sp = jnp.maximum(z, 0.0) + jnp.log(1.0 + jnp.exp2(nabs)) * LOG2E
            if masked:
                sp = jnp.where(keep, sp, 0.0)
            sp16.append(sp.astype(BF16))
            zs.append(z - sp)
            halves.append((sp[:, :LANES] + sp[:, LANES:]).astype(BF16))
        tot = _dot(jnp.concatenate(halves, axis=1), total_mat)
        return sp16, zs, tot

    def suffix(sp16):
        return [_dot(sp16[h], suffix_mat) for h in heads]

    def accumulate(base, zs, suf, tot, masked):
        vb = v_ref[0, pl.ds(pl.multiple_of(base, tk), tk), :]
        vh = [jnp.where(head0, vb, zero), jnp.where(head0, zero, vb)]
        pv = None
        for h in heads:
            carry = carry_ref[h]
            att = jnp.exp2(zs[h] - suf[h] - jnp.concatenate([carry] * (tk // LANES), axis=1))
            if masked:
                att = jnp.where(keep, att, 0.0)
            d = _dot(att.astype(BF16), vh[h])
            pv = d if pv is None else pv + d
            carry_ref[h] = carry + tot[:, h * LANES:(h + 1) * LANES]
        acc_ref[...] += pv

    def stash(slot, sp16, zs, tot):
        for h in heads:
            sp_ref[slot, h] = sp16[h]
            zs_ref[slot, h] = zs[h]
        tot_ref[slot] = tot

    carry_ref[...] = jnp.zeros_like(carry_ref)
    acc_ref[...] = jnp.zeros_like(acc_ref)

    sp16, zs, tot = logits(i * tq, True)
    stash(0, *logits(jnp.maximum(i - 1, 0) * tq, False))
    accumulate(i * tq, zs, suffix(sp16), tot, True)

    def body(j, c):
        prev = (j - 1) & 1
        suf = suffix([sp_ref[prev, h] for h in heads])
        stash(j & 1, *logits((i - 1 - j) * tq, False))
        accumulate((i - j) * tq, [zs_ref[prev, h] for h in heads], suf, tot_ref[prev], False)
        return c

    lax.fori_loop(1, i, body, 0)

    @pl.when(i > 0)
    def _():
        last = (i - 1) & 1
        accumulate(0, [zs_ref[last, h] for h in heads], suffix([sp_ref[last, h] for h in heads]),
                   tot_ref[last], False)

    o_ref[0] = acc_ref[...].astype(o_ref.dtype)


def _sb_attn(q, kv, tq=256):
    bsz, seq, c = q.shape
    assert seq % tq == 0 and tq == 2 * LANES
    nhb = c // LANES
    kspec = pl.BlockSpec((1, seq, LANES), lambda b, h, i: (b, 0, h))
    vspec = pl.BlockSpec((1, seq, LANES), lambda b, h, i: (b, 0, nhb + h))
    qo = pl.BlockSpec((1, tq, LANES), lambda b, h, i: (b, i, h))
    return pl.pallas_call(
        functools.partial(_sb_attn_kernel, tq=tq),
        out_shape=jax.ShapeDtypeStruct((bsz, seq, c), BF16),
        grid=(bsz, c // LANES, seq // tq),
        in_specs=[qo, kspec, vspec],
        out_specs=qo,
        scratch_shapes=[pltpu.VMEM((HEADS_PER_BLOCK, tq, LANES), F32), pltpu.VMEM((tq, LANES), F32),
                        pltpu.VMEM((2, HEADS_PER_BLOCK, tq, tq), BF16),
                        pltpu.VMEM((2, HEADS_PER_BLOCK, tq, tq), F32),
                        pltpu.VMEM((2, tq, HEADS_PER_BLOCK * LANES), F32)],
        compiler_params=_params("parallel", "parallel", "arbitrary"),
        name="sb_attn",
    )(q, kv, kv)


def _router_params(w_group, b_group, w_inner, b_inner):
    c = w_group.shape[0]
    w = jnp.concatenate([w_group, w_inner.reshape(c, N_EXPERTS)], axis=1)
    b = jnp.concatenate([b_group, b_inner.reshape(N_EXPERTS)])
    pad = LANES - w.shape[1]
    return jnp.pad(w, ((0, 0), (0, pad))), jnp.pad(b, (0, pad)).reshape(1, LANES)


def kernel(x, norm_mix, norm_ffn, norm_kv, norm_final, rw_mix, rw_w_rkv, rw_w0, rw_w1, rw_w2, rw_a0, rw_a1, rw_a2, rw_g1, rw_g2, rw_k_k, rw_k_a, rw_r_k, rw_lnx_w, rw_lnx_b, rw_w_out, w_kv, sb_w_q, sb_w_out, moe_w_group, moe_b_group, moe_w_inner, moe_b_inner, moe_w_gate, moe_w_up, moe_w_down):
    bsz, seq, c = x.shape
    depth = norm_mix.shape[0]
    n_a = rw_mix.shape[0]
    m = bsz * seq
    vec = lambda t: t.reshape(1, c)
    bf = lambda t: t.astype(BF16)

    h = x.reshape(m, c)
    kv_sh = None
    for l in range(depth):
        if l < n_a:
            i = l
            r, k, v, lw, kk, a, g = _rwkv_pre(
                h, seq, vec(norm_mix[l]), jnp.pad(rw_mix[i], ((0, 2), (0, 0))), bf(rw_w_rkv[i]),
                vec(rw_w0[i]), bf(rw_w1[i]), bf(rw_w2[i]), vec(rw_a0[i]), bf(rw_a1[i]), bf(rw_a2[i]),
                bf(rw_g1[i]), bf(rw_g2[i]), vec(rw_k_k[i]), vec(rw_k_a[i]))
            s3 = lambda t: t.reshape(bsz, seq, c)
            y = _wkv(s3(r), s3(lw), s3(k), s3(v), s3(kk), s3(a), vec(rw_r_k[i]),
                     vec(rw_lnx_w[i]), vec(rw_lnx_b[i]))
            h = _proj_res(y.reshape(m, c), g, bf(rw_w_out[i]), h)
        else:
            j = l - n_a
            q = _norm_proj(h, vec(norm_mix[l]), bf(sb_w_q[j]), LOG2E * HEAD_DIM ** -0.5)
            o = _sb_attn(q.reshape(bsz, seq, c), kv_sh)
            h = _proj_res(o.reshape(m, c), None, bf(sb_w_out[j]), h)
        w_r, b_r = _router_params(moe_w_group[l], moe_b_group[l], moe_w_inner[l], moe_b_inner[l])
        h = _moe(h, vec(norm_ffn[l]), w_r, b_r, bf(moe_w_gate[l]), bf(moe_w_up[l]),
                 bf(moe_w_down[l]), vec(norm_final), l == depth - 1)
        if l == n_a - 1:
            kv_sh = _norm_proj(h, vec(norm_kv), bf(w_kv), 1.0).reshape(bsz, seq, 2 * c)
    return h.reshape(bsz, seq, c)
```

```python
import functools

import jax
import jax.numpy as jnp
from jax import lax
from jax.experimental import pallas as pl
from jax.experimental.pallas import tpu as pltpu

F32 = jnp.float32
BF16 = jnp.bfloat16

LANES = 128
HEAD_DIM = 64
HEADS_PER_BLOCK = LANES // HEAD_DIM
RMS_EPS = 1e-6
GN_EPS = 64e-5
N_GROUPS = 4
EXPERTS_PER_GROUP = 4
N_EXPERTS = N_GROUPS * EXPERTS_PER_GROUP
WKV_CHUNK = 64
VMEM_LIMIT = 56 * 1024 * 1024

HIGHEST = lax.Precision.HIGHEST
LOG2E = 1.4426950408889634


def _params(*sem):
    return pltpu.CompilerParams(dimension_semantics=sem, vmem_limit_bytes=VMEM_LIMIT)


def _dot(a, b, precision=None):
    return lax.dot_general(a, b, (((1,), (0,)), ((), ())), precision=precision,
                           preferred_element_type=F32)


def _dot_nt(a, b, precision=None):
    return lax.dot_general(a, b, (((1,), (1,)), ((), ())), precision=precision,
                           preferred_element_type=F32)


def _dot_tn(a, b, precision=None):
    return lax.dot_general(a, b, (((0,), (0,)), ((), ())), precision=precision,
                           preferred_element_type=F32)


def _bf16_terms(x, n):
    terms = []
    for _ in range(n):
        t = x.astype(BF16)
        terms.append(t)
        x = x - t.astype(F32)
    return terms


def _dot_mask_rhs(x, mask01, n):
    m = mask01.astype(BF16)
    return sum(_dot(t, m) for t in _bf16_terms(x, n))


def _dot_mask_lhs(mask01, x, n):
    m = mask01.astype(BF16)
    return sum(_dot(m, t) for t in _bf16_terms(x, n))


def _rms(x):
    return x * lax.rsqrt(jnp.mean(x * x, axis=-1, keepdims=True) + RMS_EPS)


def _sigmoid(x):
    return 1.0 / (1.0 + jnp.exp(-x))


def _softplus(x):
    return jnp.maximum(x, 0.0) + jnp.log(1.0 + jnp.exp(-jnp.abs(x)))


def _full(shape):
    nd = len(shape)
    return pl.BlockSpec(shape, lambda *_: (0,) * nd)


def _rwkv_pre_kernel(h_ref, hp_ref, gn_ref, mix_ref, wrkv_ref, w0_ref, w1_ref, w2_ref,
                     a0_ref, a1_ref, a2_ref, g1_ref, g2_ref, kk_ref, ka_ref,
                     r_out, k_out, v_out, lw_out, kk_out, a_out, g_out, *, tiles_per_seq):
    i = pl.program_id(0)
    gn = gn_ref[...]
    hn = _rms(h_ref[...]) * gn
    hpn = _rms(hp_ref[7:8, :]) * gn
    hpn = jnp.where(i % tiles_per_seq == 0, 0.0, hpn)
    row = lax.broadcasted_iota(jnp.int32, hn.shape, 0)
    prev = jnp.where(row == 0, hpn, pltpu.roll(hn, 1, axis=0))
    xx = prev - hn

    def mixed(n):
        return (hn + xx * mix_ref[n:n + 1, :]).astype(BF16)

    r = _dot(mixed(0), wrkv_ref[0])
    k = _dot(mixed(1), wrkv_ref[1])
    v = _dot(mixed(2), wrkv_ref[2])
    wl = w0_ref[...] + _dot(jnp.tanh(_dot(mixed(3), w1_ref[...])).astype(BF16), w2_ref[...])
    a = _sigmoid(a0_ref[...] + _dot(_dot(mixed(4), a1_ref[...]).astype(BF16), a2_ref[...]))
    g = _dot(_sigmoid(_dot(mixed(5), g1_ref[...])).astype(BF16), g2_ref[...])
    r_out[...] = r
    k_out[...] = k * (1.0 + (a - 1.0) * ka_ref[...])
    v_out[...] = v
    lw_out[...] = -jnp.exp(-_softplus(-wl) - 0.5)
    kk_out[...] = k * kk_ref[...]
    a_out[...] = a
    g_out[...] = g


def _rwkv_pre(h, seq, gn, mix, wrkv, w0, w1, w2, a0, a1, a2, g1, g2, k_k, k_a, tm=256):
    m, c = h.shape
    assert seq % tm == 0 and m % seq == 0
    row = lambda i: (i, 0)
    out = jax.ShapeDtypeStruct((m, c), F32)
    ins = [gn, mix, wrkv, w0, w1, w2, a0, a1, a2, g1, g2, k_k, k_a]
    return pl.pallas_call(
        functools.partial(_rwkv_pre_kernel, tiles_per_seq=seq // tm),
        out_shape=[out] * 7,
        grid=(m // tm,),
        in_specs=[pl.BlockSpec((tm, c), row),
                  pl.BlockSpec((8, c), lambda i: (jnp.maximum(i * (tm // 8) - 1, 0), 0))]
                 + [_full(t.shape) for t in ins],
        out_specs=[pl.BlockSpec((tm, c), row)] * 7,
        compiler_params=_params("parallel"),
        name="rwkv_pre",
    )(h, h, *ins)


def _wkv_kernel(r_ref, lw_ref, k_ref, v_ref, kk_ref, a_ref, rk_ref, gw_ref, gb_ref,
                y_ref, st_ref, *, chunks):
    L = WKV_CHUNK

    @pl.when(pl.program_id(2) == 0)
    def _():
        st_ref[...] = jnp.zeros_like(st_ref)

    tl = chunks * L
    lane = lax.broadcasted_iota(jnp.int32, (1, LANES), 1)
    head0 = lane < HEAD_DIM
    rr = lax.broadcasted_iota(jnp.int32, (LANES, LANES), 0)
    cc = lax.broadcasted_iota(jnp.int32, (LANES, LANES), 1)
    bd = ((rr < HEAD_DIM) == (cc < HEAD_DIM)).astype(F32)
    eye = rr == cc
    tr = lax.broadcasted_iota(jnp.int32, (tl, tl), 0)
    tc = lax.broadcasted_iota(jnp.int32, (tl, tl), 1)
    shift = L.bit_length() - 1
    same_chunk = (tr >> shift) == (tc >> shift)
    strict = same_chunk & (tr > tc)
    incl = same_chunk & (tr >= tc)
    tri = incl.astype(F32)
    eye_t = (tr == tc).astype(F32)
    m0 = head0.astype(F32)
    masks = (m0, 1.0 - m0)
    head0w = jnp.concatenate([head0, head0], axis=1)
    sel = lambda x0, x1: jnp.where(head0 if x0.shape[1] == LANES else head0w, x0, x1)

    r = r_ref[0]
    lw = lw_ref[0]
    k = k_ref[0]
    v = v_ref[0]
    kkr = kk_ref[0]
    ag = a_ref[0]

    cum = _dot_mask_lhs(tri, lw, 3)
    cum_end = [cum[(c + 1) * L - 1:(c + 1) * L, :] for c in range(chunks)]
    cum_l = jnp.concatenate([jnp.broadcast_to(e, (L, LANES)) for e in cum_end], axis=0)
    w_in = jnp.exp(cum)
    w_ex = jnp.exp(cum - lw)
    w_inv = jnp.exp(-cum)
    w_tail = jnp.exp(cum_l - cum)
    kk = kkr / jnp.maximum(jnp.sqrt(_dot_mask_rhs(kkr * kkr, bd, 1)), 1e-12)
    b = kk * ag
    at = -kk * w_ex
    at16 = at.astype(BF16)
    bt16 = (b * w_inv).astype(BF16)
    kt16 = (k * w_inv).astype(BF16)
    rt = r * w_in
    bh16 = (b * w_tail).astype(BF16)
    kh16 = (k * w_tail).astype(BF16)
    v16 = v.astype(BF16)

    heads = range(HEADS_PER_BLOCK)
    a_ab, a_ak, a_rb, a_rk = [], [], [], []
    for h in heads:
        lhs = jnp.concatenate([at * masks[h], rt * masks[h]], axis=0).astype(BF16)
        pb = _dot_nt(lhs, bt16)
        pk = _dot_nt(lhs, kt16)
        a_ab.append(jnp.where(strict, pb[:tl], 0.0))
        a_ak.append(jnp.where(strict, pk[:tl], 0.0).astype(BF16))
        a_rb.append(jnp.where(incl, pb[tl:], 0.0).astype(BF16))
        a_rk.append(jnp.where(incl, pk[tl:], 0.0).astype(BF16))
    t = [eye_t + a for a in a_ab]
    p = [a.astype(BF16) for a in a_ab]
    akv = [_dot(a_ak[h], v16).astype(BF16) for h in heads]
    rkv = [_dot(a_rk[h], v16) for h in heads]
    for _ in range(5):
        p = [_dot(p[h], p[h]).astype(BF16) for h in heads]
        t = [t[h] + _dot(t[h].astype(BF16), p[h]) for h in heads]
    tau = [_dot(t[h].astype(BF16), jnp.concatenate([at16, akv[h]], axis=1)).astype(BF16) for h in heads]
    rpy = [_dot(a_rb[h], tau[h]) for h in heads]
    tau16 = sel(*tau)
    ta16 = tau16[:, :LANES]
    u16 = tau16[:, LANES:]
    rpy = sel(*rpy)
    rp16 = (rt + rpy[:, :LANES]).astype(BF16)
    yi = rpy[:, LANES:] + sel(*rkv)

    mts, gts = [], []
    for c in range(chunks):
        rows = slice(c * L, (c + 1) * L)
        mt = jnp.where(eye, jnp.exp(cum_end[c]), 0.0) + bd * _dot_tn(bh16[rows], ta16[rows])
        mts.append(mt.astype(BF16))
        gts.append(bd * (_dot_tn(bh16[rows], u16[rows]) + _dot_tn(kh16[rows], v16[rows])))
    st = st_ref[...]
    ys = []
    for c in range(chunks):
        rows = slice(c * L, (c + 1) * L)
        st16 = st.astype(BF16)
        ys.append(_dot(rp16[rows], st16) + yi[rows])
        st = _dot(mts[c], st16) + gts[c]
    st_ref[...] = st
    y = jnp.concatenate(ys, axis=0)

    mu = _dot_mask_rhs(y, bd, 1) * (1.0 / HEAD_DIM)
    d = y - mu
    var = _dot_mask_rhs(d * d, bd, 1) * (1.0 / HEAD_DIM)
    yn = d * lax.rsqrt(var + GN_EPS) * gw_ref[...] + gb_ref[...]
    bonus = _dot_mask_rhs(r * k * rk_ref[...], bd, 1) * v
    y_ref[0] = yn + bonus


def _wkv(r, lw, k, v, kk, a, r_k, lnx_w, lnx_b, chunks=4):
    bsz, seq, c = r.shape
    tl = chunks * WKV_CHUNK
    assert seq % tl == 0 and c % LANES == 0
    tok = pl.BlockSpec((1, tl, LANES), lambda b, h, t: (b, t, h))
    par = pl.BlockSpec((1, LANES), lambda b, h, t: (0, h))
    return pl.pallas_call(
        functools.partial(_wkv_kernel, chunks=chunks),
        out_shape=jax.ShapeDtypeStruct((bsz, seq, c), F32),
        grid=(bsz, c // LANES, seq // tl),
        in_specs=[tok] * 6 + [par] * 3,
        out_specs=tok,
        scratch_shapes=[pltpu.VMEM((LANES, LANES), F32)],
        compiler_params=_params("parallel", "parallel", "arbitrary"),
        name="wkv",
    )(r, lw, k, v, kk, a, r_k, lnx_w, lnx_b)


def _proj_res_kernel(*refs, gated):
    if gated:
        x_ref, g_ref, w_ref, res_ref, o_ref = refs
        x = (x_ref[...] * g_ref[...]).astype(BF16)
    else:
        x_ref, w_ref, res_ref, o_ref = refs
        x = x_ref[...]
    o_ref[...] = res_ref[...] + _dot(x, w_ref[...])


def _proj_res(x, gate, w, res, tm=512):
    m, c = res.shape
    row = pl.BlockSpec((tm, c), lambda i: (i, 0))
    gated = gate is not None
    ins = [x, gate, w, res] if gated else [x, w, res]
    specs = [row, row, _full(w.shape), row] if gated else [row, _full(w.shape), row]
    return pl.pallas_call(
        functools.partial(_proj_res_kernel, gated=gated),
        out_shape=jax.ShapeDtypeStruct((m, c), F32),
        grid=(m // tm,),
        in_specs=specs,
        out_specs=row,
        compiler_params=_params("parallel"),
        name="proj_res",
    )(*ins)


def _moe_kernel(h_ref, gn_ref, wr_ref, br_ref, wg_ref, wu_ref, wd_ref, gf_ref, o_ref,
                xn_ref, gates_ref, acc_ref, *, final_norm):
    e = pl.program_id(1)
    lane = lax.broadcasted_iota(jnp.int32, (1, LANES), 1)

    @pl.when(e == 0)
    def _():
        xn = _rms(h_ref[...]) * gn_ref[...]
        xn_ref[...] = xn.astype(BF16)
        logits = _dot(xn, wr_ref[...], HIGHEST) + br_ref[...]
        neg = jnp.float32(-jnp.inf)
        is_g = lane < N_GROUPS
        gl = jnp.where(is_g, logits, neg)
        gmax = jnp.max(gl, axis=-1, keepdims=True)
        gval = 1.0 / jnp.sum(jnp.exp(gl - gmax), axis=-1, keepdims=True)
        gidx = jnp.min(jnp.where(gl == gmax, lane, LANES), axis=-1, keepdims=True)
        ex = lane - N_GROUPS
        in_grp = (ex >= gidx * EXPERTS_PER_GROUP) & (ex < (gidx + 1) * EXPERTS_PER_GROUP)
        il = jnp.where(in_grp, logits, neg)
        l1 = jnp.max(il, axis=-1, keepdims=True)
        i1 = jnp.min(jnp.where(il == l1, lane, LANES), axis=-1, keepdims=True)
        il2 = jnp.where(lane == i1, neg, il)
        l2 = jnp.max(il2, axis=-1, keepdims=True)
        i2 = jnp.min(jnp.where(il2 == l2, lane, LANES), axis=-1, keepdims=True)
        e2 = jnp.exp(l2 - l1)
        w1 = gval / (1.0 + e2)
        w2 = gval * e2 / (1.0 + e2)
        gates_ref[...] = jnp.where(lane == i1, w1, 0.0) + jnp.where(lane == i2, w2, 0.0)
        acc_ref[...] = jnp.zeros_like(acc_ref)

    xn = xn_ref[...]
    gate_e = jnp.sum(jnp.where(lane == e + N_GROUPS, gates_ref[...], 0.0), axis=-1, keepdims=True)
    hg = _dot(xn, wg_ref[0])
    hu = _dot(xn, wu_ref[0])
    hm = (hg * _sigmoid(hg)) * hu * gate_e
    acc_ref[...] += _dot(hm.astype(BF16), wd_ref[0])

    @pl.when(e == N_EXPERTS - 1)
    def _():
        out = h_ref[...] + acc_ref[...]
        if final_norm:
            out = _rms(out) * gf_ref[...]
        o_ref[...] = out


def _moe(h, gn, w_router, b_router, w_gate, w_up, w_down, g_final, final_norm, tm=512):
    m, c = h.shape
    f = w_gate.shape[-1]
    row = pl.BlockSpec((tm, c), lambda i, e: (i, 0))
    return pl.pallas_call(
        functools.partial(_moe_kernel, final_norm=final_norm),
        out_shape=jax.ShapeDtypeStruct((m, c), F32),
        grid=(m // tm, N_EXPERTS),
        in_specs=[row, _full(gn.shape), _full(w_router.shape), _full(b_router.shape),
                  pl.BlockSpec((1, c, f), lambda i, e: (e, 0, 0)),
                  pl.BlockSpec((1, c, f), lambda i, e: (e, 0, 0)),
                  pl.BlockSpec((1, f, c), lambda i, e: (e, 0, 0)),
                  _full(g_final.shape)],
        out_specs=row,
        scratch_shapes=[pltpu.VMEM((tm, c), BF16), pltpu.VMEM((tm, LANES), F32),
                        pltpu.VMEM((tm, c), F32)],
        compiler_params=_params("parallel", "arbitrary"),
        name="moe",
    )(h, gn, w_router, b_router, w_gate, w_up, w_down, g_final)


def _norm_proj_kernel(h_ref, gn_ref, w_ref, o_ref, *, scale):
    xn = (_rms(h_ref[...]) * gn_ref[...]).astype(BF16)
    o_ref[...] = (_dot(xn, w_ref[...]) * scale).astype(o_ref.dtype)


def _norm_proj(h, gn, w, scale, tm=512):
    m, c = h.shape
    n = w.shape[1]
    return pl.pallas_call(
        functools.partial(_norm_proj_kernel, scale=scale),
        out_shape=jax.ShapeDtypeStruct((m, n), BF16),
        grid=(m // tm,),
        in_specs=[pl.BlockSpec((tm, c), lambda i: (i, 0)), _full(gn.shape), _full(w.shape)],
        out_specs=pl.BlockSpec((tm, n), lambda i: (i, 0)),
        compiler_params=_params("parallel"),
        name="norm_proj",
    )(h, gn, w)


def _sb_attn_kernel(q_ref, k_ref, v_ref, o_ref, carry_ref, acc_ref, sp_ref, zs_ref, tot_ref, *, tq):
    i = pl.program_id(2)
    tk = tq
    lane = lax.broadcasted_iota(jnp.int32, (1, LANES), 1)
    head0 = lane < HEAD_DIM
    jr = lax.broadcasted_iota(jnp.int32, (tk, tk), 0)
    sc = lax.broadcasted_iota(jnp.int32, (tk, tk), 1)
    suffix_mat = (jr > sc).astype(BF16)
    jt = lax.broadcasted_iota(jnp.int32, (2 * LANES, 2 * LANES), 0)
    st = lax.broadcasted_iota(jnp.int32, (2 * LANES, 2 * LANES), 1)
    total_mat = ((jt < LANES) == (st < LANES)).astype(BF16)

    q = q_ref[0]
    zero = jnp.zeros((), q.dtype)
    qh = [jnp.where(head0, q, zero), jnp.where(head0, zero, q)]
    sign = jnp.uint32(0x80000000)

    heads = range(HEADS_PER_BLOCK)
    keep = sc < jr

    def logits(base, masked):
        kb = k_ref[0, pl.ds(pl.multiple_of(base, tk), tk), :]
        sp16, zs, halves = [], [], []
        for h in heads:
            z = _dot_nt(qh[h], kb)
            nabs = lax.bitcast_convert_type(lax.bitcast_convert_type(z, jnp.uint32) | sign, F32)
            sp = jnp.maximum(z, 0.0) + jnp.log(1.0 + jnp.exp2(nabs)) * LOG2E
            if masked:
                sp = jnp.where(keep, sp, 0.0)
            sp16.append(sp.astype(BF16))
            zs.append(z - sp)
            halves.append((sp[:, :LANES] + sp[:, LANES:]).astype(BF16))
        tot = _dot(jnp.concatenate(halves, axis=1), total_mat)
        return sp16, zs, tot

    def suffix(sp16):
        return [_dot(sp16[h], suffix_mat) for h in heads]

    def accumulate(base, zs, suf, tot, masked):
        vb = v_ref[0, pl.ds(pl.multiple_of(base, tk), tk), :]
        vh = [jnp.where(head0, vb, zero), jnp.where(head0, zero, vb)]
        pv = None
        for h in heads:
            carry = carry_ref[h]
            att = jnp.exp2(zs[h] - suf[h] - jnp.concatenate([carry] * (tk // LANES), axis=1))
            if masked:
                att = jnp.where(keep, att, 0.0)
            d = _dot(att.astype(BF16), vh[h])
            pv = d if pv is None else pv + d
            carry_ref[h] = carry + tot[:, h * LANES:(h + 1) * LANES]
        acc_ref[...] += pv

    def stash(slot, sp16, zs, tot):
        for h in heads:
            sp_ref[slot, h] = sp16[h]
            zs_ref[slot, h] = zs[h]
        tot_ref[slot] = tot

    carry_ref[...] = jnp.zeros_like(carry_ref)
    acc_ref[...] = jnp.zeros_like(acc_ref)

    sp16, zs, tot = logits(i * tq, True)
    stash(0, *logits(jnp.maximum(i - 1, 0) * tq, False))
    accumulate(i * tq, zs, suffix(sp16), tot, True)

    def body(j, c):
        prev = (j - 1) & 1
        suf = suffix([sp_ref[prev, h] for h in heads])
        stash(j & 1, *logits((i - 1 - j) * tq, False))
        accumulate((i - j) * tq, [zs_ref[prev, h] for h in heads], suf, tot_ref[prev], False)
        return c

    lax.fori_loop(1, i, body, 0)

    @pl.when(i > 0)
    def _():
        last = (i - 1) & 1
        accumulate(0, [zs_ref[last, h] for h in heads], suffix([sp_ref[last, h] for h in heads]),
                   tot_ref[last], False)

    o_ref[0] = acc_ref[...].astype(o_ref.dtype)


def _sb_attn(q, kv, tq=256):
    bsz, seq, c = q.shape
    assert seq % tq == 0 and tq == 2 * LANES
    nhb = c // LANES
    kspec = pl.BlockSpec((1, seq, LANES), lambda b, h, i: (b, 0, h))
    vspec = pl.BlockSpec((1, seq, LANES), lambda b, h, i: (b, 0, nhb + h))
    qo = pl.BlockSpec((1, tq, LANES), lambda b, h, i: (b, i, h))
    return pl.pallas_call(
        functools.partial(_sb_attn_kernel, tq=tq),
        out_shape=jax.ShapeDtypeStruct((bsz, seq, c), BF16),
        grid=(bsz, c // LANES, seq // tq),
        in_specs=[qo, kspec, vspec],
        out_specs=qo,
        scratch_shapes=[pltpu.VMEM((HEADS_PER_BLOCK, tq, LANES), F32), pltpu.VMEM((tq, LANES), F32),
                        pltpu.VMEM((2, HEADS_PER_BLOCK, tq, tq), BF16),
                        pltpu.VMEM((2, HEADS_PER_BLOCK, tq, tq), F32),
                        pltpu.VMEM((2, tq, HEADS_PER_BLOCK * LANES), F32)],
        compiler_params=_params("parallel", "parallel", "arbitrary"),
        name="sb_attn",
    )(q, kv, kv)


MASK_BIAS = -1e30
SOFTPLUS_LINEAR = 100.0


def _sb_flat_kernel(q_ref, k_ref, v_ref, o_ref, z_ref, e_ref, tot_ref, att_ref, carry_ref, acc_ref,
                    bias_ref, qh_ref, vh_ref, *, tq, nq):
    tk = tq
    heads = range(HEADS_PER_BLOCK)
    lane = lax.broadcasted_iota(jnp.int32, (1, LANES), 1)
    head0 = lane < HEAD_DIM
    jr = lax.broadcasted_iota(jnp.int32, (tk, tk), 0)
    sc = lax.broadcasted_iota(jnp.int32, (tk, tk), 1)
    suffix_mat = (jr > sc).astype(BF16)
    ones_mat = jnp.ones((tk, LANES), BF16)
    zero = jnp.zeros((), BF16)

    def rows(ref, idx, n):
        return ref[pl.ds(pl.multiple_of(idx * n, n), n), :]

    for ref in (z_ref, e_ref, tot_ref, att_ref, carry_ref, acc_ref):
        ref[...] = jnp.zeros_like(ref)
    bias_ref[0] = jnp.zeros((tq, tk), F32)
    bias_ref[1] = jnp.where(sc < jr, 0.0, MASK_BIAS)
    q = q_ref[0]
    v = v_ref[0]
    qh_ref[0] = jnp.where(head0, q, zero)
    qh_ref[1] = jnp.where(head0, zero, q)
    vh_ref[0] = jnp.where(head0, v, zero)
    vh_ref[1] = jnp.where(head0, zero, v)

    def stage0(t, kv, slot):
        kb = rows(k_ref.at[0], kv, tk)
        bias = bias_ref[(kv == t).astype(jnp.int32)]
        for h in heads:
            z_ref[slot, h] = _dot_nt(rows(qh_ref.at[h], t, tq), kb) + bias

    def stage1_head(slot):
        sp16, zs = [], []
        for h in heads:
            z = z_ref[slot, h]
            sp = jnp.maximum(z, jnp.log(1.0 + jnp.exp2(jnp.minimum(z, SOFTPLUS_LINEAR))) * LOG2E)
            sp16.append(sp.astype(BF16))
            zs.append(z - sp)
        return sp16, zs

    def stage1_tail(slot, sp16, zs):
        for h in heads:
            e_ref[slot, h] = zs[h] - _dot(sp16[h], suffix_mat)
            tot_ref[slot, h] = _dot(sp16[h], ones_mat)

    def stage2(t, kv, slot):
        first = kv == t
        for h in heads:
            carry = jnp.where(first, 0.0, carry_ref[h])
            att = jnp.exp2(e_ref[slot, h] - jnp.concatenate([carry] * (tk // LANES), axis=1))
            att_ref[slot, h] = att.astype(BF16)
            carry_ref[h] = carry + tot_ref[slot, h]

    def stage3(t, kv, slot):
        pv = (_dot(att_ref[slot, 0], rows(vh_ref.at[0], kv, tk))
              + _dot(att_ref[slot, 1], rows(vh_ref.at[1], kv, tk)))
        acc = jnp.where(kv == t, 0.0, acc_ref[...]) + pv
        acc_ref[...] = acc
        o_ref[0, pl.ds(pl.multiple_of(t * tq, tq), tq), :] = acc.astype(o_ref.dtype)

    def step(cur, blocks):
        (t0, k0), (t1, k1), (t2, k2), (t3, k3) = blocks
        old = 1 - cur
        stage0(t0, k0, cur)
        stage3(t3, k3, old)
        held = stage1_head(old)
        stage2(t2, k2, cur)
        stage1_tail(old, *held)
        last = k0 == 0
        nt = jnp.minimum(jnp.where(last, t0 + 1, t0), nq - 1)
        nk = jnp.where(last, nt, k0 - 1)
        return ((nt, nk), (t0, k0), (t1, k1), (t2, k2))

    z32 = jnp.int32(0)
    n_steps = nq * (nq + 1) // 2 + 3
    blocks = lax.fori_loop(0, n_steps // 2, lambda _, b: step(1, step(0, b)), ((z32, z32),) * 4)
    if n_steps % 2:
        step(0, blocks)


def _sb_attn_flat(q, kv, tq=256):
    bsz, seq, c = q.shape
    assert seq % tq == 0 and tq == 2 * LANES
    nhb = c // LANES
    qspec = pl.BlockSpec((1, seq, LANES), lambda b, h: (b, 0, h))
    vspec = pl.BlockSpec((1, seq, LANES), lambda b, h: (b, 0, nhb + h))
    pair = (2, HEADS_PER_BLOCK, tq, tq)
    return pl.pallas_call(
        functools.partial(_sb_flat_kernel, tq=tq, nq=seq // tq),
        out_shape=jax.ShapeDtypeStruct((bsz, seq, c), BF16),
        grid=(bsz, nhb),
        in_specs=[qspec, qspec, vspec],
        out_specs=qspec,
        scratch_shapes=[pltpu.VMEM(pair, F32), pltpu.VMEM(pair, F32),
                        pltpu.VMEM((2, HEADS_PER_BLOCK, tq, LANES), F32), pltpu.VMEM(pair, BF16),
                        pltpu.VMEM((HEADS_PER_BLOCK, tq, LANES), F32), pltpu.VMEM((tq, LANES), F32),
                        pltpu.VMEM((2, tq, tq), F32),
                        pltpu.VMEM((HEADS_PER_BLOCK, seq, LANES), BF16),
                        pltpu.VMEM((HEADS_PER_BLOCK, seq, LANES), BF16)],
        compiler_params=_params("parallel", "parallel"),
        name="sb_attn",
    )(q, kv, kv)


def _router_params(w_group, b_group, w_inner, b_inner):
    c = w_group.shape[0]
    w = jnp.concatenate([w_group, w_inner.reshape(c, N_EXPERTS)], axis=1)
    b = jnp.concatenate([b_group, b_inner.reshape(N_EXPERTS)])
    pad = LANES - w.shape[1]
    return jnp.pad(w, ((0, 0), (0, pad))), jnp.pad(b, (0, pad)).reshape(1, LANES)


def kernel(x, norm_mix, norm_ffn, norm_kv, norm_final, rw_mix, rw_w_rkv, rw_w0, rw_w1, rw_w2, rw_a0, rw_a1, rw_a2, rw_g1, rw_g2, rw_k_k, rw_k_a, rw_r_k, rw_lnx_w, rw_lnx_b, rw_w_out, w_kv, sb_w_q, sb_w_out, moe_w_group, moe_b_group, moe_w_inner, moe_b_inner, moe_w_gate, moe_w_up, moe_w_down):
    bsz, seq, c = x.shape
    depth = norm_mix.shape[0]
    n_a = rw_mix.shape[0]
    m = bsz * seq
    vec = lambda t: t.reshape(1, c)
    bf = lambda t: t.astype(BF16)

    h = x.reshape(m, c)
    kv_sh = None
    for l in range(depth):
        if l < n_a:
            i = l
            r, k, v, lw, kk, a, g = _rwkv_pre(
                h, seq, vec(norm_mix[l]), jnp.pad(rw_mix[i], ((0, 2), (0, 0))), bf(rw_w_rkv[i]),
                vec(rw_w0[i]), bf(rw_w1[i]), bf(rw_w2[i]), vec(rw_a0[i]), bf(rw_a1[i]), bf(rw_a2[i]),
                bf(rw_g1[i]), bf(rw_g2[i]), vec(rw_k_k[i]), vec(rw_k_a[i]))
            s3 = lambda t: t.reshape(bsz, seq, c)
            y = _wkv(s3(r), s3(lw), s3(k), s3(v), s3(kk), s3(a), vec(rw_r_k[i]),
                     vec(rw_lnx_w[i]), vec(rw_lnx_b[i]))
            h = _proj_res(y.reshape(m, c), g, bf(rw_w_out[i]), h)
        else:
            j = l - n_a
            q = _norm_proj(h, vec(norm_mix[l]), bf(sb_w_q[j]), LOG2E * HEAD_DIM ** -0.5)
            o = _sb_attn_flat(q.reshape(bsz, seq, c), kv_sh)
            h = _proj_res(o.reshape(m, c), None, bf(sb_w_out[j]), h)
        w_r, b_r = _router_params(moe_w_group[l], moe_b_group[l], moe_w_inner[l], moe_b_inner[l])
        h = _moe(h, vec(norm_ffn[l]), w_r, b_r, bf(moe_w_gate[l]), bf(moe_w_up[l]),
                 bf(moe_w_down[l]), vec(norm_final), l == depth - 1)
        if l == n_a - 1:
            kv_sh = _norm_proj(h, vec(norm_kv), bf(w_kv), 1.0).reshape(bsz, seq, 2 * c)
    return h.reshape(bsz, seq, c)
```

```python
import functools

import jax
import jax.numpy as jnp
from jax import lax
from jax.experimental import pallas as pl
from jax.experimental.pallas import tpu as pltpu

F32 = jnp.float32
BF16 = jnp.bfloat16

LANES = 128
HEAD_DIM = 64
HEADS_PER_BLOCK = LANES // HEAD_DIM
RMS_EPS = 1e-6
GN_EPS = 64e-5
N_GROUPS = 4
EXPERTS_PER_GROUP = 4
N_EXPERTS = N_GROUPS * EXPERTS_PER_GROUP
WKV_CHUNK = 64
VMEM_LIMIT = 56 * 1024 * 1024
LOG2E = 1.4426950408889634

TM_RWKV_PRE = 256
TM_PROJ = 512
TM_MOE = 512
WKV_CHUNKS_PER_STEP = 4
WKV_PAIRS_PER_STEP = 2
ATTN_TILE = 2 * LANES


def _params(*sem):
    return pltpu.CompilerParams(dimension_semantics=sem, vmem_limit_bytes=VMEM_LIMIT)


def _dot(a, b):
    return lax.dot_general(a, b, (((1,), (0,)), ((), ())), preferred_element_type=F32)


def _dot_nt(a, b):
    return lax.dot_general(a, b, (((1,), (1,)), ((), ())), preferred_element_type=F32)


def _dot_tn(a, b):
    return lax.dot_general(a, b, (((0,), (0,)), ((), ())), preferred_element_type=F32)


def _bf16_terms(x, n):
    terms = []
    for _ in range(n):
        t = x.astype(BF16)
        terms.append(t)
        x = x - t.astype(F32)
    return terms


def _dot_mask_rhs(x, mask01, n):
    m = mask01.astype(BF16)
    return sum(_dot(t, m) for t in _bf16_terms(x, n))


def _dot_mask_lhs(mask01, x, n):
    m = mask01.astype(BF16)
    return sum(_dot(m, t) for t in _bf16_terms(x, n))


def _dot_3pass(a, b):
    a_hi, a_lo = _bf16_terms(a, 2)
    b_hi, b_lo = _bf16_terms(b, 2)
    return _dot(a_hi, b_hi) + _dot(a_hi, b_lo) + _dot(a_lo, b_hi)


def _rms(x):
    return x * lax.rsqrt(jnp.mean(x * x, axis=-1, keepdims=True) + RMS_EPS)


def _sigmoid(x):
    return 1.0 / (1.0 + jnp.exp(-x))


def _softplus(x):
    return jnp.maximum(x, 0.0) + jnp.log(1.0 + jnp.exp(-jnp.abs(x)))


def _full(shape):
    nd = len(shape)
    return pl.BlockSpec(shape, lambda *_: (0,) * nd)


def _rwkv_pre_kernel(h_ref, hp_ref, gn_ref, mix_ref, wrkv_ref, w0_ref, w1_ref, w2_ref,
                     a0_ref, a1_ref, a2_ref, g1_ref, g2_ref, kk_ref, ka_ref,
                     r_out, k_out, v_out, lw_out, kk_out, a_out, g_out, *, tiles_per_seq):
    i = pl.program_id(0)
    gn = gn_ref[...]
    hn = _rms(h_ref[...]) * gn
    hpn = _rms(hp_ref[7:8, :]) * gn
    hpn = jnp.where(i % tiles_per_seq == 0, 0.0, hpn)
    row = lax.broadcasted_iota(jnp.int32, hn.shape, 0)
    prev = jnp.where(row == 0, hpn, pltpu.roll(hn, 1, axis=0))
    xx = prev - hn

    def mixed(n):
        return (hn + xx * mix_ref[n:n + 1, :]).astype(BF16)

    r = _dot(mixed(0), wrkv_ref[0])
    k = _dot(mixed(1), wrkv_ref[1])
    v = _dot(mixed(2), wrkv_ref[2])
    wl = w0_ref[...] + _dot(jnp.tanh(_dot(mixed(3), w1_ref[...])).astype(BF16), w2_ref[...])
    a = _sigmoid(a0_ref[...] + _dot(_dot(mixed(4), a1_ref[...]).astype(BF16), a2_ref[...]))
    g = _dot(_sigmoid(_dot(mixed(5), g1_ref[...])).astype(BF16), g2_ref[...])
    r_out[...] = r
    k_out[...] = k * (1.0 + (a - 1.0) * ka_ref[...])
    v_out[...] = v
    lw_out[...] = -jnp.exp(-_softplus(-wl) - 0.5)
    kk_out[...] = k * kk_ref[...]
    a_out[...] = a
    g_out[...] = g


def _rwkv_pre(h, seq, gn, mix, wrkv, w0, w1, w2, a0, a1, a2, g1, g2, k_k, k_a):
    m, c = h.shape
    tm = TM_RWKV_PRE
    assert seq % tm == 0 and m % seq == 0
    row = lambda i: (i, 0)
    out = jax.ShapeDtypeStruct((m, c), F32)
    ins = [gn, mix, wrkv, w0, w1, w2, a0, a1, a2, g1, g2, k_k, k_a]
    return pl.pallas_call(
        functools.partial(_rwkv_pre_kernel, tiles_per_seq=seq // tm),
        out_shape=[out] * 7,
        grid=(m // tm,),
        in_specs=[pl.BlockSpec((tm, c), row),
                  pl.BlockSpec((8, c), lambda i: (jnp.maximum(i * (tm // 8) - 1, 0), 0))]
                 + [_full(t.shape) for t in ins],
        out_specs=[pl.BlockSpec((tm, c), row)] * 7,
        compiler_params=_params("parallel"),
        name="rwkv_pre",
    )(h, h, *ins)


def _wkv_kernel(r_ref, lw_ref, k_ref, v_ref, kk_ref, a_ref, rk_ref, gw_ref, gb_ref,
                y_ref, st_ref, *, chunks, pairs):
    L = WKV_CHUNK

    @pl.when(pl.program_id(2) == 0)
    def _():
        st_ref[...] = jnp.zeros_like(st_ref)

    tl = chunks * L
    lane = lax.broadcasted_iota(jnp.int32, (1, LANES), 1)
    head0 = lane < HEAD_DIM
    rr = lax.broadcasted_iota(jnp.int32, (LANES, LANES), 0)
    cc = lax.broadcasted_iota(jnp.int32, (LANES, LANES), 1)
    bd = ((rr < HEAD_DIM) == (cc < HEAD_DIM)).astype(F32)
    eye = rr == cc
    tr = lax.broadcasted_iota(jnp.int32, (tl, tl), 0)
    tc = lax.broadcasted_iota(jnp.int32, (tl, tl), 1)
    shift = L.bit_length() - 1
    same_chunk = (tr >> shift) == (tc >> shift)
    strict = same_chunk & (tr > tc)
    incl = same_chunk & (tr >= tc)
    tri = incl.astype(F32)
    eye_t = (tr == tc).astype(F32)
    m0 = head0.astype(F32)
    masks = (m0, 1.0 - m0)
    head0w = jnp.concatenate([head0, head0], axis=1)
    sel = lambda x0, x1: jnp.where(head0 if x0.shape[1] == LANES else head0w, x0, x1)

    prs = range(pairs)
    heads = range(HEADS_PER_BLOCK)
    units = [(p, h) for p in prs for h in heads]
    lanes = [slice(p * LANES, (p + 1) * LANES) for p in prs]

    r, v, k, at, at16, bt16, kt16, rt, bh16, kh16, v16, cum_end = ([] for _ in range(12))
    for p in prs:
        r_p = r_ref[0, :, lanes[p]]
        lw = lw_ref[0, :, lanes[p]]
        k_p = k_ref[0, :, lanes[p]]
        v_p = v_ref[0, :, lanes[p]]
        kkr = kk_ref[0, :, lanes[p]]
        ag = a_ref[0, :, lanes[p]]
        cum = _dot_mask_lhs(tri, lw, 3)
        ends = [cum[(c + 1) * L - 1:(c + 1) * L, :] for c in range(chunks)]
        cum_l = jnp.concatenate([jnp.broadcast_to(e, (L, LANES)) for e in ends], axis=0)
        w_inv = jnp.exp(-cum)
        w_tail = jnp.exp(cum_l - cum)
        kk = kkr / jnp.maximum(jnp.sqrt(_dot_mask_rhs(kkr * kkr, bd, 1)), 1e-12)
        b = kk * ag
        at_p = -kk * jnp.exp(cum - lw)
        r.append(r_p)
        k.append(k_p)
        v.append(v_p)
        cum_end.append(ends)
        at.append(at_p)
        at16.append(at_p.astype(BF16))
        bt16.append((b * w_inv).astype(BF16))
        kt16.append((k_p * w_inv).astype(BF16))
        rt.append(r_p * jnp.exp(cum))
        bh16.append((b * w_tail).astype(BF16))
        kh16.append((k_p * w_tail).astype(BF16))
        v16.append(v_p.astype(BF16))

    a_ab, a_ak, a_rb, a_rk = [], [], [], []
    for p, h in units:
        lhs = jnp.concatenate([at[p] * masks[h], rt[p] * masks[h]], axis=0).astype(BF16)
        pb = _dot_nt(lhs, bt16[p])
        pk = _dot_nt(lhs, kt16[p])
        a_ab.append(jnp.where(strict, pb[:tl], 0.0))
        a_ak.append(jnp.where(strict, pk[:tl], 0.0).astype(BF16))
        a_rb.append(jnp.where(incl, pb[tl:], 0.0).astype(BF16))
        a_rk.append(jnp.where(incl, pk[tl:], 0.0).astype(BF16))
    n_units = range(len(units))
    t = [eye_t + a for a in a_ab]
    pw = [a.astype(BF16) for a in a_ab]
    akv = [_dot(a_ak[u], v16[units[u][0]]).astype(BF16) for u in n_units]
    rkv = [_dot(a_rk[u], v16[units[u][0]]) for u in n_units]
    for _ in range(L.bit_length() - 2):
        pw = [_dot(pw[u], pw[u]).astype(BF16) for u in n_units]
        t = [t[u] + _dot(t[u].astype(BF16), pw[u]) for u in n_units]
    tau = [_dot(t[u].astype(BF16), jnp.concatenate([at16[units[u][0]], akv[u]], axis=1)).astype(BF16)
           for u in n_units]
    rpy = [_dot(a_rb[u], tau[u]) for u in n_units]

    ta16, u16, rp16, yi = [], [], [], []
    for p in prs:
        u0 = p * HEADS_PER_BLOCK
        tau_p = sel(tau[u0], tau[u0 + 1])
        rpy_p = sel(rpy[u0], rpy[u0 + 1])
        ta16.append(tau_p[:, :LANES])
        u16.append(tau_p[:, LANES:])
        rp16.append((rt[p] + rpy_p[:, :LANES]).astype(BF16))
        yi.append(rpy_p[:, LANES:] + sel(rkv[u0], rkv[u0 + 1]))

    mts, gts = [], []
    for c in range(chunks):
        rows = slice(c * L, (c + 1) * L)
        mts.append([(jnp.where(eye, jnp.exp(cum_end[p][c]), 0.0)
                     + bd * _dot_tn(bh16[p][rows], ta16[p][rows])).astype(BF16) for p in prs])
        gts.append([bd * (_dot_tn(bh16[p][rows], u16[p][rows]) + _dot_tn(kh16[p][rows], v16[p][rows]))
                    for p in prs])
    st = [st_ref[p] for p in prs]
    ys = [[] for _ in prs]
    for c in range(chunks):
        rows = slice(c * L, (c + 1) * L)
        st16 = [s.astype(BF16) for s in st]
        for p in prs:
            ys[p].append(_dot(rp16[p][rows], st16[p]) + yi[p][rows])
        st = [_dot(mts[c][p], st16[p]) + gts[c][p] for p in prs]
    for p in prs:
        st_ref[p] = st[p]
        y = jnp.concatenate(ys[p], axis=0)
        mu = _dot_mask_rhs(y, bd, 1) * (1.0 / HEAD_DIM)
        d = y - mu
        var = _dot_mask_rhs(d * d, bd, 1) * (1.0 / HEAD_DIM)
        yn = d * lax.rsqrt(var + GN_EPS) * gw_ref[:, lanes[p]] + gb_ref[:, lanes[p]]
        bonus = _dot_mask_rhs(r[p] * k[p] * rk_ref[:, lanes[p]], bd, 1) * v[p]
        y_ref[0, :, lanes[p]] = yn + bonus


def _wkv(r, lw, k, v, kk, a, r_k, lnx_w, lnx_b):
    bsz, seq, c = r.shape
    chunks, pairs = WKV_CHUNKS_PER_STEP, WKV_PAIRS_PER_STEP
    tl = chunks * WKV_CHUNK
    width = pairs * LANES
    assert seq % tl == 0 and c % width == 0
    tok = pl.BlockSpec((1, tl, width), lambda b, h, t: (b, t, h))
    par = pl.BlockSpec((1, width), lambda b, h, t: (0, h))
    return pl.pallas_call(
        functools.partial(_wkv_kernel, chunks=chunks, pairs=pairs),
        out_shape=jax.ShapeDtypeStruct((bsz, seq, c), F32),
        grid=(bsz, c // width, seq // tl),
        in_specs=[tok] * 6 + [par] * 3,
        out_specs=tok,
        scratch_shapes=[pltpu.VMEM((pairs, LANES, LANES), F32)],
        compiler_params=_params("parallel", "parallel", "arbitrary"),
        name="wkv",
    )(r, lw, k, v, kk, a, r_k, lnx_w, lnx_b)


def _proj_res_kernel(*refs, gated):
    if gated:
        x_ref, g_ref, w_ref, res_ref, o_ref = refs
        x = (x_ref[...] * g_ref[...]).astype(BF16)
    else:
        x_ref, w_ref, res_ref, o_ref = refs
        x = x_ref[...]
    o_ref[...] = res_ref[...] + _dot(x, w_ref[...])


def _proj_res(x, gate, w, res):
    m, c = res.shape
    tm = TM_PROJ
    row = pl.BlockSpec((tm, c), lambda i: (i, 0))
    gated = gate is not None
    ins = [x, gate, w, res] if gated else [x, w, res]
    specs = [row, row, _full(w.shape), row] if gated else [row, _full(w.shape), row]
    return pl.pallas_call(
        functools.partial(_proj_res_kernel, gated=gated),
        out_shape=jax.ShapeDtypeStruct((m, c), F32),
        grid=(m // tm,),
        in_specs=specs,
        out_specs=row,
        compiler_params=_params("parallel"),
        name="proj_res",
    )(*ins)


def _moe_kernel(h_ref, gn_ref, wr_ref, br_ref, wg_ref, wu_ref, wd_ref, gf_ref, o_ref,
                xn_ref, gates_ref, acc_ref, *, final_norm, d_expert):
    g = pl.program_id(1)
    lane = lax.broadcasted_iota(jnp.int32, (1, LANES), 1)

    @pl.when(g == 0)
    def _():
        xn = _rms(h_ref[...]) * gn_ref[...]
        xn_ref[...] = xn.astype(BF16)
        logits = _dot_3pass(xn, wr_ref[...]) + br_ref[...]
        neg = jnp.float32(-jnp.inf)
        is_g = lane < N_GROUPS
        gl = jnp.where(is_g, logits, neg)
        gmax = jnp.max(gl, axis=-1, keepdims=True)
        gval = 1.0 / jnp.sum(jnp.exp(gl - gmax), axis=-1, keepdims=True)
        gidx = jnp.min(jnp.where(gl == gmax, lane, LANES), axis=-1, keepdims=True)
        ex = lane - N_GROUPS
        in_grp = (ex >= gidx * EXPERTS_PER_GROUP) & (ex < (gidx + 1) * EXPERTS_PER_GROUP)
        il = jnp.where(in_grp, logits, neg)
        l1 = jnp.max(il, axis=-1, keepdims=True)
        i1 = jnp.min(jnp.where(il == l1, lane, LANES), axis=-1, keepdims=True)
        il2 = jnp.where(lane == i1, neg, il)
        l2 = jnp.max(il2, axis=-1, keepdims=True)
        i2 = jnp.min(jnp.where(il2 == l2, lane, LANES), axis=-1, keepdims=True)
        e2 = jnp.exp(l2 - l1)
        w1 = gval / (1.0 + e2)
        w2 = gval * e2 / (1.0 + e2)
        gates_ref[...] = jnp.where(lane == i1, w1, 0.0) + jnp.where(lane == i2, w2, 0.0)
        acc_ref[...] = jnp.zeros_like(acc_ref)

    xn = xn_ref[...]
    gates = gates_ref[...]
    hg = _dot(xn, wg_ref[0])
    hu = _dot(xn, wu_ref[0])
    first = N_GROUPS + g * EXPERTS_PER_GROUP
    cols = []
    for e in range(EXPERTS_PER_GROUP):
        ge = jnp.sum(jnp.where(lane == first + e, gates, 0.0), axis=-1, keepdims=True)
        cols.append(jnp.broadcast_to(ge, (ge.shape[0], d_expert)))
    hm = (hg * _sigmoid(hg)) * hu * jnp.concatenate(cols, axis=1)
    acc_ref[...] += _dot(hm.astype(BF16), wd_ref[0])

    @pl.when(g == N_GROUPS - 1)
    def _():
        out = h_ref[...] + acc_ref[...]
        if final_norm:
            out = _rms(out) * gf_ref[...]
        o_ref[...] = out


def _moe(h, gn, w_router, b_router, w_gate, w_up, w_down, g_final, final_norm):
    m, c = h.shape
    tm = TM_MOE
    gf = w_gate.shape[-1]
    row = pl.BlockSpec((tm, c), lambda i, g: (i, 0))
    return pl.pallas_call(
        functools.partial(_moe_kernel, final_norm=final_norm, d_expert=gf // EXPERTS_PER_GROUP),
        out_shape=jax.ShapeDtypeStruct((m, c), F32),
        grid=(m // tm, N_GROUPS),
        in_specs=[row, _full(gn.shape), _full(w_router.shape), _full(b_router.shape),
                  pl.BlockSpec((1, c, gf), lambda i, g: (g, 0, 0)),
                  pl.BlockSpec((1, c, gf), lambda i, g: (g, 0, 0)),
                  pl.BlockSpec((1, gf, c), lambda i, g: (g, 0, 0)),
                  _full(g_final.shape)],
        out_specs=row,
        scratch_shapes=[pltpu.VMEM((tm, c), BF16), pltpu.VMEM((tm, LANES), F32),
                        pltpu.VMEM((tm, c), F32)],
        compiler_params=_params("parallel", "arbitrary"),
        name="moe",
    )(h, gn, w_router, b_router, w_gate, w_up, w_down, g_final)


def _norm_proj_kernel(h_ref, gn_ref, w_ref, o_ref, *, scale):
    xn = (_rms(h_ref[...]) * gn_ref[...]).astype(BF16)
    o_ref[...] = (_dot(xn, w_ref[...]) * scale).astype(o_ref.dtype)


def _norm_proj(h, gn, w, scale):
    m, c = h.shape
    tm = TM_PROJ
    n = w.shape[1]
    return pl.pallas_call(
        functools.partial(_norm_proj_kernel, scale=scale),
        out_shape=jax.ShapeDtypeStruct((m, n), BF16),
        grid=(m // tm,),
        in_specs=[pl.BlockSpec((tm, c), lambda i: (i, 0)), _full(gn.shape), _full(w.shape)],
        out_specs=pl.BlockSpec((tm, n), lambda i: (i, 0)),
        compiler_params=_params("parallel"),
        name="norm_proj",
    )(h, gn, w)


MASK_BIAS = -1e30
SOFTPLUS_LINEAR = 100.0


def _sb_attn_kernel(q_ref, k_ref, v_ref, o_ref, z_ref, e_ref, tot_ref, att_ref, carry_ref, acc_ref,
                    bias_ref, qh_ref, vh_ref, *, tq, nq):
    tk = tq
    heads = range(HEADS_PER_BLOCK)
    lane = lax.broadcasted_iota(jnp.int32, (1, LANES), 1)
    head0 = lane < HEAD_DIM
    jr = lax.broadcasted_iota(jnp.int32, (tk, tk), 0)
    sc = lax.broadcasted_iota(jnp.int32, (tk, tk), 1)
    suffix_mat = (jr > sc).astype(BF16)
    ones_mat = jnp.ones((tk, LANES), BF16)
    zero = jnp.zeros((), BF16)

    def rows(ref, idx, n):
        return ref[pl.ds(pl.multiple_of(idx * n, n), n), :]

    for ref in (z_ref, e_ref, tot_ref, att_ref, carry_ref, acc_ref):
        ref[...] = jnp.zeros_like(ref)
    bias_ref[0] = jnp.zeros((tq, tk), F32)
    bias_ref[1] = jnp.where(sc < jr, 0.0, MASK_BIAS)
    q = q_ref[0]
    v = v_ref[0]
    qh_ref[0] = jnp.where(head0, q, zero)
    qh_ref[1] = jnp.where(head0, zero, q)
    vh_ref[0] = jnp.where(head0, v, zero)
    vh_ref[1] = jnp.where(head0, zero, v)

    def stage0(t, kv, slot):
        kb = rows(k_ref.at[0], kv, tk)
        bias = bias_ref[(kv == t).astype(jnp.int32)]
        for h in heads:
            z_ref[slot, h] = _dot_nt(rows(qh_ref.at[h], t, tq), kb) + bias

    def stage1_head(slot):
        sp16, zs = [], []
        for h in heads:
            z = z_ref[slot, h]
            sp = jnp.maximum(z, jnp.log(1.0 + jnp.exp2(jnp.minimum(z, SOFTPLUS_LINEAR))) * LOG2E)
            sp16.append(sp.astype(BF16))
            zs.append(z - sp)
        return sp16, zs

    def stage1_tail(slot, sp16, zs):
        for h in heads:
            e_ref[slot, h] = zs[h] - _dot(sp16[h], suffix_mat)
            tot_ref[slot, h] = _dot(sp16[h], ones_mat)

    def stage2(t, kv, slot):
        first = kv == t
        for h in heads:
            carry = jnp.where(first, 0.0, carry_ref[h])
            att = jnp.exp2(e_ref[slot, h] - jnp.concatenate([carry] * (tk // LANES), axis=1))
            att_ref[slot, h] = att.astype(BF16)
            carry_ref[h] = carry + tot_ref[slot, h]

    def stage3(t, kv, slot):
        pv = (_dot(att_ref[slot, 0], rows(vh_ref.at[0], kv, tk))
              + _dot(att_ref[slot, 1], rows(vh_ref.at[1], kv, tk)))
        acc = jnp.where(kv == t, 0.0, acc_ref[...]) + pv
        acc_ref[...] = acc
        o_ref[0, pl.ds(pl.multiple_of(t * tq, tq), tq), :] = acc.astype(o_ref.dtype)

    def step(cur, blocks):
        (t0, k0), (t1, k1), (t2, k2), (t3, k3) = blocks
        old = 1 - cur
        stage0(t0, k0, cur)
        stage3(t3, k3, old)
        held = stage1_head(old)
        stage2(t2, k2, cur)
        stage1_tail(old, *held)
        last = k0 == 0
        nt = jnp.minimum(jnp.where(last, t0 + 1, t0), nq - 1)
        nk = jnp.where(last, nt, k0 - 1)
        return ((nt, nk), (t0, k0), (t1, k1), (t2, k2))

    z32 = jnp.int32(0)
    n_steps = nq * (nq + 1) // 2 + 3
    blocks = lax.fori_loop(0, n_steps // 2, lambda _, b: step(1, step(0, b)), ((z32, z32),) * 4)
    if n_steps % 2:
        step(0, blocks)


def _sb_attn(q, kv):
    bsz, seq, c = q.shape
    tq = ATTN_TILE
    assert seq % tq == 0
    nhb = c // LANES
    qspec = pl.BlockSpec((1, seq, LANES), lambda b, h: (b, 0, h))
    vspec = pl.BlockSpec((1, seq, LANES), lambda b, h: (b, 0, nhb + h))
    pair = (2, HEADS_PER_BLOCK, tq, tq)
    return pl.pallas_call(
        functools.partial(_sb_attn_kernel, tq=tq, nq=seq // tq),
        out_shape=jax.ShapeDtypeStruct((bsz, seq, c), BF16),
        grid=(bsz, nhb),
        in_specs=[qspec, qspec, vspec],
        out_specs=qspec,
        scratch_shapes=[pltpu.VMEM(pair, F32), pltpu.VMEM(pair, F32),
                        pltpu.VMEM((2, HEADS_PER_BLOCK, tq, LANES), F32), pltpu.VMEM(pair, BF16),
                        pltpu.VMEM((HEADS_PER_BLOCK, tq, LANES), F32), pltpu.VMEM((tq, LANES), F32),
                        pltpu.VMEM((2, tq, tq), F32),
                        pltpu.VMEM((HEADS_PER_BLOCK, seq, LANES), BF16),
                        pltpu.VMEM((HEADS_PER_BLOCK, seq, LANES), BF16)],
        compiler_params=_params("parallel", "parallel"),
        name="sb_attn",
    )(q, kv, kv)


def _router_params(w_group, b_group, w_inner, b_inner):
    c = w_group.shape[0]
    w = jnp.concatenate([w_group, w_inner.reshape(c, N_EXPERTS)], axis=1)
    b = jnp.concatenate([b_group, b_inner.reshape(N_EXPERTS)])
    pad = LANES - w.shape[1]
    return jnp.pad(w, ((0, 0), (0, pad))), jnp.pad(b, (0, pad)).reshape(1, LANES)


def _group_columns(w):
    _, c, f = w.shape
    w = w.reshape(N_GROUPS, EXPERTS_PER_GROUP, c, f).transpose(0, 2, 1, 3)
    return w.reshape(N_GROUPS, c, EXPERTS_PER_GROUP * f).astype(BF16)


def _group_rows(w):
    _, f, c = w.shape
    return w.reshape(N_GROUPS, EXPERTS_PER_GROUP * f, c).astype(BF16)


def kernel(x, norm_mix, norm_ffn, norm_kv, norm_final, rw_mix, rw_w_rkv, rw_w0, rw_w1, rw_w2, rw_a0, rw_a1, rw_a2, rw_g1, rw_g2, rw_k_k, rw_k_a, rw_r_k, rw_lnx_w, rw_lnx_b, rw_w_out, w_kv, sb_w_q, sb_w_out, moe_w_group, moe_b_group, moe_w_inner, moe_b_inner, moe_w_gate, moe_w_up, moe_w_down):
    bsz, seq, c = x.shape
    depth = norm_mix.shape[0]
    n_a = rw_mix.shape[0]
    m = bsz * seq
    vec = lambda t: t.reshape(1, c)
    bf = lambda t: t.astype(BF16)

    h = x.reshape(m, c)
    kv_sh = None
    for l in range(depth):
        if l < n_a:
            i = l
            r, k, v, lw, kk, a, g = _rwkv_pre(
                h, seq, vec(norm_mix[l]), jnp.pad(rw_mix[i], ((0, 2), (0, 0))), bf(rw_w_rkv[i]),
                vec(rw_w0[i]), bf(rw_w1[i]), bf(rw_w2[i]), vec(rw_a0[i]), bf(rw_a1[i]), bf(rw_a2[i]),
                bf(rw_g1[i]), bf(rw_g2[i]), vec(rw_k_k[i]), vec(rw_k_a[i]))
            s3 = lambda t: t.reshape(bsz, seq, c)
            y = _wkv(s3(r), s3(lw), s3(k), s3(v), s3(kk), s3(a), vec(rw_r_k[i]),
                     vec(rw_lnx_w[i]), vec(rw_lnx_b[i]))
            h = _proj_res(y.reshape(m, c), g, bf(rw_w_out[i]), h)
        else:
            j = l - n_a
            q = _norm_proj(h, vec(norm_mix[l]), bf(sb_w_q[j]), LOG2E * HEAD_DIM ** -0.5)
            o = _sb_attn(q.reshape(bsz, seq, c), kv_sh)
            h = _proj_res(o.reshape(m, c), None, bf(sb_w_out[j]), h)
        w_r, b_r = _router_params(moe_w_group[l], moe_b_group[l], moe_w_inner[l], moe_b_inner[l])
        h = _moe(h, vec(norm_ffn[l]), w_r, b_r, _group_columns(moe_w_gate[l]), _group_columns(moe_w_up[l]),
                 _group_rows(moe_w_down[l]), vec(norm_final), l == depth - 1)
        if l == n_a - 1:
            kv_sh = _norm_proj(h, vec(norm_kv), bf(w_kv), 1.0).reshape(bsz, seq, 2 * c)
    return h.reshape(bsz, seq, c)
```

```python
import functools

import jax
import jax.numpy as jnp
from jax import lax
from jax.experimental import pallas as pl
from jax.experimental.pallas import tpu as pltpu

F32 = jnp.float32
BF16 = jnp.bfloat16

LANES = 128
HEAD_DIM = 64
HEADS_PER_BLOCK = LANES // HEAD_DIM
RMS_EPS = 1e-6
GN_EPS = 64e-5
N_GROUPS = 4
EXPERTS_PER_GROUP = 4
N_EXPERTS = N_GROUPS * EXPERTS_PER_GROUP
WKV_CHUNK = 64
VMEM_LIMIT = 56 * 1024 * 1024
LOG2E = 1.4426950408889634

TM_RWKV_PRE = 256
TM_PROJ = 512
TM_MOE = 512
WKV_CHUNKS_PER_STEP = 4
WKV_PAIRS_PER_STEP = 4
ATTN_TILE = 2 * LANES
ATTN_STEPS_PER_TRIP = 2


def _params(*sem):
    return pltpu.CompilerParams(dimension_semantics=sem, vmem_limit_bytes=VMEM_LIMIT)


def _dot(a, b):
    return lax.dot_general(a, b, (((1,), (0,)), ((), ())), preferred_element_type=F32)


def _dot_nt(a, b):
    return lax.dot_general(a, b, (((1,), (1,)), ((), ())), preferred_element_type=F32)


def _dot_tn(a, b):
    return lax.dot_general(a, b, (((0,), (0,)), ((), ())), preferred_element_type=F32)


def _bf16_terms(x, n):
    terms = []
    for _ in range(n):
        t = x.astype(BF16)
        terms.append(t)
        x = x - t.astype(F32)
    return terms


def _dot_mask_rhs(x, mask01, n):
    m = mask01.astype(BF16)
    return sum(_dot(t, m) for t in _bf16_terms(x, n))


def _dot_mask_lhs(mask01, x, n):
    m = mask01.astype(BF16)
    return sum(_dot(m, t) for t in _bf16_terms(x, n))


def _dot_3pass(a, b):
    a_hi, a_lo = _bf16_terms(a, 2)
    b_hi, b_lo = _bf16_terms(b, 2)
    return _dot(a_hi, b_hi) + _dot(a_hi, b_lo) + _dot(a_lo, b_hi)


def _rms(x):
    return x * lax.rsqrt(jnp.mean(x * x, axis=-1, keepdims=True) + RMS_EPS)


def _sigmoid(x):
    return 1.0 / (1.0 + jnp.exp(-x))


def _softplus(x):
    return jnp.maximum(x, 0.0) + jnp.log(1.0 + jnp.exp(-jnp.abs(x)))


def _full(shape):
    nd = len(shape)
    return pl.BlockSpec(shape, lambda *_: (0,) * nd)


def _rwkv_pre_kernel(h_ref, hp_ref, gn_ref, mix_ref, wrkv_ref, w0_ref, w1_ref, w2_ref,
                     a0_ref, a1_ref, a2_ref, g1_ref, g2_ref, kk_ref, ka_ref,
                     r_out, k_out, v_out, lw_out, kk_out, a_out, g_out, *, tiles_per_seq):
    i = pl.program_id(0)
    gn = gn_ref[...]
    hn = _rms(h_ref[...]) * gn
    hpn = _rms(hp_ref[7:8, :]) * gn
    hpn = jnp.where(i % tiles_per_seq == 0, 0.0, hpn)
    row = lax.broadcasted_iota(jnp.int32, hn.shape, 0)
    prev = jnp.where(row == 0, hpn, pltpu.roll(hn, 1, axis=0))
    xx = prev - hn

    def mixed(n):
        return (hn + xx * mix_ref[n:n + 1, :]).astype(BF16)

    r = _dot(mixed(0), wrkv_ref[0])
    k = _dot(mixed(1), wrkv_ref[1])
    v = _dot(mixed(2), wrkv_ref[2])
    wl = w0_ref[...] + _dot(jnp.tanh(_dot(mixed(3), w1_ref[...])).astype(BF16), w2_ref[...])
    a = _sigmoid(a0_ref[...] + _dot(_dot(mixed(4), a1_ref[...]).astype(BF16), a2_ref[...]))
    g = _dot(_sigmoid(_dot(mixed(5), g1_ref[...])).astype(BF16), g2_ref[...])
    r_out[...] = r
    k_out[...] = k * (1.0 + (a - 1.0) * ka_ref[...])
    v_out[...] = v
    lw_out[...] = -jnp.exp(-_softplus(-wl) - 0.5)
    kk_out[...] = k * kk_ref[...]
    a_out[...] = a
    g_out[...] = g


def _rwkv_pre(h, seq, gn, mix, wrkv, w0, w1, w2, a0, a1, a2, g1, g2, k_k, k_a):
    m, c = h.shape
    tm = TM_RWKV_PRE
    assert seq % tm == 0 and m % seq == 0
    row = lambda i: (i, 0)
    out = jax.ShapeDtypeStruct((m, c), F32)
    ins = [gn, mix, wrkv, w0, w1, w2, a0, a1, a2, g1, g2, k_k, k_a]
    return pl.pallas_call(
        functools.partial(_rwkv_pre_kernel, tiles_per_seq=seq // tm),
        out_shape=[out] * 7,
        grid=(m // tm,),
        in_specs=[pl.BlockSpec((tm, c), row),
                  pl.BlockSpec((8, c), lambda i: (jnp.maximum(i * (tm // 8) - 1, 0), 0))]
                 + [_full(t.shape) for t in ins],
        out_specs=[pl.BlockSpec((tm, c), row)] * 7,
        compiler_params=_params("parallel"),
        name="rwkv_pre",
    )(h, h, *ins)


def _wkv_kernel(r_ref, lw_ref, k_ref, v_ref, kk_ref, a_ref, rk_ref, gw_ref, gb_ref,
                y_ref, st_ref, *, chunks, pairs):
    L = WKV_CHUNK

    @pl.when(pl.program_id(2) == 0)
    def _():
        st_ref[...] = jnp.zeros_like(st_ref)

    tl = chunks * L
    lane = lax.broadcasted_iota(jnp.int32, (1, LANES), 1)
    head0 = lane < HEAD_DIM
    rr = lax.broadcasted_iota(jnp.int32, (LANES, LANES), 0)
    cc = lax.broadcasted_iota(jnp.int32, (LANES, LANES), 1)
    bd = ((rr < HEAD_DIM) == (cc < HEAD_DIM)).astype(F32)
    eye = rr == cc
    tr = lax.broadcasted_iota(jnp.int32, (tl, tl), 0)
    tc = lax.broadcasted_iota(jnp.int32, (tl, tl), 1)
    shift = L.bit_length() - 1
    same_chunk = (tr >> shift) == (tc >> shift)
    strict = same_chunk & (tr > tc)
    incl = same_chunk & (tr >= tc)
    tri = incl.astype(F32)
    eye_t = (tr == tc).astype(F32)
    m0 = head0.astype(F32)
    masks = (m0, 1.0 - m0)
    head0w = jnp.concatenate([head0, head0], axis=1)
    sel = lambda x0, x1: jnp.where(head0 if x0.shape[1] == LANES else head0w, x0, x1)

    prs = range(pairs)
    heads = range(HEADS_PER_BLOCK)
    units = [(p, h) for p in prs for h in heads]
    lanes = [slice(p * LANES, (p + 1) * LANES) for p in prs]

    r, v, k, at, at16, bt16, kt16, rt, bh16, kh16, v16, cum_end = ([] for _ in range(12))
    for p in prs:
        r_p = r_ref[0, :, lanes[p]]
        lw = lw_ref[0, :, lanes[p]]
        k_p = k_ref[0, :, lanes[p]]
        v_p = v_ref[0, :, lanes[p]]
        kkr = kk_ref[0, :, lanes[p]]
        ag = a_ref[0, :, lanes[p]]
        cum = _dot_mask_lhs(tri, lw, 3)
        ends = [cum[(c + 1) * L - 1:(c + 1) * L, :] for c in range(chunks)]
        cum_l = jnp.concatenate([jnp.broadcast_to(e, (L, LANES)) for e in ends], axis=0)
        w_inv = jnp.exp(-cum)
        w_tail = jnp.exp(cum_l - cum)
        kk = kkr / jnp.maximum(jnp.sqrt(_dot_mask_rhs(kkr * kkr, bd, 1)), 1e-12)
        b = kk * ag
        at_p = -kk * jnp.exp(cum - lw)
        r.append(r_p)
        k.append(k_p)
        v.append(v_p)
        cum_end.append(ends)
        at.append(at_p)
        at16.append(at_p.astype(BF16))
        bt16.append((b * w_inv).astype(BF16))
        kt16.append((k_p * w_inv).astype(BF16))
        rt.append(r_p * jnp.exp(cum))
        bh16.append((b * w_tail).astype(BF16))
        kh16.append((k_p * w_tail).astype(BF16))
        v16.append(v_p.astype(BF16))

    a_ab, a_ak, a_rb, a_rk = [], [], [], []
    for p, h in units:
        lhs = jnp.concatenate([at[p] * masks[h], rt[p] * masks[h]], axis=0).astype(BF16)
        pb = _dot_nt(lhs, bt16[p])
        pk = _dot_nt(lhs, kt16[p])
        a_ab.append(jnp.where(strict, pb[:tl], 0.0))
        a_ak.append(jnp.where(strict, pk[:tl], 0.0).astype(BF16))
        a_rb.append(jnp.where(incl, pb[tl:], 0.0).astype(BF16))
        a_rk.append(jnp.where(incl, pk[tl:], 0.0).astype(BF16))
    n_units = range(len(units))
    t = [eye_t + a for a in a_ab]
    pw = [a.astype(BF16) for a in a_ab]
    akv = [_dot(a_ak[u], v16[units[u][0]]).astype(BF16) for u in n_units]
    rkv = [_dot(a_rk[u], v16[units[u][0]]) for u in n_units]
    for _ in range(L.bit_length() - 2):
        pw = [_dot(pw[u], pw[u]).astype(BF16) for u in n_units]
        t = [t[u] + _dot(t[u].astype(BF16), pw[u]) for u in n_units]
    tau = [_dot(t[u].astype(BF16), jnp.concatenate([at16[units[u][0]], akv[u]], axis=1)).astype(BF16)
           for u in n_units]
    rpy = [_dot(a_rb[u], tau[u]) for u in n_units]

    ta16, u16, rp16, yi = [], [], [], []
    for p in prs:
        u0 = p * HEADS_PER_BLOCK
        tau_p = sel(tau[u0], tau[u0 + 1])
        rpy_p = sel(rpy[u0], rpy[u0 + 1])
        ta16.append(tau_p[:, :LANES])
        u16.append(tau_p[:, LANES:])
        rp16.append((rt[p] + rpy_p[:, :LANES]).astype(BF16))
        yi.append(rpy_p[:, LANES:] + sel(rkv[u0], rkv[u0 + 1]))

    mts, gts = [], []
    for c in range(chunks):
        rows = slice(c * L, (c + 1) * L)
        mts.append([(jnp.where(eye, jnp.exp(cum_end[p][c]), 0.0)
                     + bd * _dot_tn(bh16[p][rows], ta16[p][rows])).astype(BF16) for p in prs])
        gts.append([bd * (_dot_tn(bh16[p][rows], u16[p][rows]) + _dot_tn(kh16[p][rows], v16[p][rows]))
                    for p in prs])
    st = [st_ref[p] for p in prs]
    ys = [[] for _ in prs]
    for c in range(chunks):
        rows = slice(c * L, (c + 1) * L)
        st16 = [s.astype(BF16) for s in st]
        for p in prs:
            ys[p].append(_dot(rp16[p][rows], st16[p]) + yi[p][rows])
        st = [_dot(mts[c][p], st16[p]) + gts[c][p] for p in prs]
    for p in prs:
        st_ref[p] = st[p]
        y = jnp.concatenate(ys[p], axis=0)
        mu = _dot_mask_rhs(y, bd, 1) * (1.0 / HEAD_DIM)
        d = y - mu
        var = _dot_mask_rhs(d * d, bd, 1) * (1.0 / HEAD_DIM)
        yn = d * lax.rsqrt(var + GN_EPS) * gw_ref[:, lanes[p]] + gb_ref[:, lanes[p]]
        bonus = _dot_mask_rhs(r[p] * k[p] * rk_ref[:, lanes[p]], bd, 1) * v[p]
        y_ref[0, :, lanes[p]] = yn + bonus


def _wkv(r, lw, k, v, kk, a, r_k, lnx_w, lnx_b):
    bsz, seq, c = r.shape
    chunks, pairs = WKV_CHUNKS_PER_STEP, WKV_PAIRS_PER_STEP
    tl = chunks * WKV_CHUNK
    width = pairs * LANES
    assert seq % tl == 0 and c % width == 0
    tok = pl.BlockSpec((1, tl, width), lambda b, h, t: (b, t, h))
    par = pl.BlockSpec((1, width), lambda b, h, t: (0, h))
    return pl.pallas_call(
        functools.partial(_wkv_kernel, chunks=chunks, pairs=pairs),
        out_shape=jax.ShapeDtypeStruct((bsz, seq, c), F32),
        grid=(bsz, c // width, seq // tl),
        in_specs=[tok] * 6 + [par] * 3,
        out_specs=tok,
        scratch_shapes=[pltpu.VMEM((pairs, LANES, LANES), F32)],
        compiler_params=_params("parallel", "parallel", "arbitrary"),
        name="wkv",
    )(r, lw, k, v, kk, a, r_k, lnx_w, lnx_b)


def _proj_res_kernel(*refs, gated):
    if gated:
        x_ref, g_ref, w_ref, res_ref, o_ref = refs
        x = (x_ref[...] * g_ref[...]).astype(BF16)
    else:
        x_ref, w_ref, res_ref, o_ref = refs
        x = x_ref[...]
    o_ref[...] = res_ref[...] + _dot(x, w_ref[...])


def _proj_res(x, gate, w, res):
    m, c = res.shape
    tm = TM_PROJ
    row = pl.BlockSpec((tm, c), lambda i: (i, 0))
    gated = gate is not None
    ins = [x, gate, w, res] if gated else [x, w, res]
    specs = [row, row, _full(w.shape), row] if gated else [row, _full(w.shape), row]
    return pl.pallas_call(
        functools.partial(_proj_res_kernel, gated=gated),
        out_shape=jax.ShapeDtypeStruct((m, c), F32),
        grid=(m // tm,),
        in_specs=specs,
        out_specs=row,
        compiler_params=_params("parallel"),
        name="proj_res",
    )(*ins)


def _moe_kernel(h_ref, gn_ref, wr_ref, br_ref, wg_ref, wu_ref, wd_ref, gf_ref, o_ref,
                xn_ref, gates_ref, acc_ref, *, final_norm, d_expert):
    g = pl.program_id(1)
    lane = lax.broadcasted_iota(jnp.int32, (1, LANES), 1)

    @pl.when(g == 0)
    def _():
        xn = _rms(h_ref[...]) * gn_ref[...]
        xn_ref[...] = xn.astype(BF16)
        logits = _dot_3pass(xn, wr_ref[...]) + br_ref[...]
        neg = jnp.float32(-jnp.inf)
        is_g = lane < N_GROUPS
        gl = jnp.where(is_g, logits, neg)
        gmax = jnp.max(gl, axis=-1, keepdims=True)
        gval = 1.0 / jnp.sum(jnp.exp(gl - gmax), axis=-1, keepdims=True)
        gidx = jnp.min(jnp.where(gl == gmax, lane, LANES), axis=-1, keepdims=True)
        ex = lane - N_GROUPS
        in_grp = (ex >= gidx * EXPERTS_PER_GROUP) & (ex < (gidx + 1) * EXPERTS_PER_GROUP)
        il = jnp.where(in_grp, logits, neg)
        l1 = jnp.max(il, axis=-1, keepdims=True)
        i1 = jnp.min(jnp.where(il == l1, lane, LANES), axis=-1, keepdims=True)
        il2 = jnp.where(lane == i1, neg, il)
        l2 = jnp.max(il2, axis=-1, keepdims=True)
        i2 = jnp.min(jnp.where(il2 == l2, lane, LANES), axis=-1, keepdims=True)
        e2 = jnp.exp(l2 - l1)
        w1 = gval / (1.0 + e2)
        w2 = gval * e2 / (1.0 + e2)
        gates_ref[...] = jnp.where(lane == i1, w1, 0.0) + jnp.where(lane == i2, w2, 0.0)
        acc_ref[...] = jnp.zeros_like(acc_ref)

    xn = xn_ref[...]
    gates = gates_ref[...]
    first = N_GROUPS + g * EXPERTS_PER_GROUP
    hm = []
    for e in range(EXPERTS_PER_GROUP):
        hg = _dot(xn, wg_ref[e])
        hu = _dot(xn, wu_ref[e])
        ge = jnp.sum(jnp.where(lane == first + e, gates, 0.0), axis=-1, keepdims=True)
        hm.append(((hg * _sigmoid(hg)) * hu * ge).astype(BF16))
    wd = wd_ref[...].reshape(EXPERTS_PER_GROUP * d_expert, wd_ref.shape[-1])
    acc_ref[...] += _dot(jnp.concatenate(hm, axis=1), wd)

    @pl.when(g == N_GROUPS - 1)
    def _():
        out = h_ref[...] + acc_ref[...]
        if final_norm:
            out = _rms(out) * gf_ref[...]
        o_ref[...] = out


def _moe(h, gn, w_router, b_router, w_gate, w_up, w_down, g_final, final_norm):
    m, c = h.shape
    tm = TM_MOE
    f = w_gate.shape[-1]
    per = EXPERTS_PER_GROUP
    row = pl.BlockSpec((tm, c), lambda i, g: (i, 0))
    return pl.pallas_call(
        functools.partial(_moe_kernel, final_norm=final_norm, d_expert=f),
        out_shape=jax.ShapeDtypeStruct((m, c), F32),
        grid=(m // tm, N_GROUPS),
        in_specs=[row, _full(gn.shape), _full(w_router.shape), _full(b_router.shape),
                  pl.BlockSpec((per, c, f), lambda i, g: (g, 0, 0)),
                  pl.BlockSpec((per, c, f), lambda i, g: (g, 0, 0)),
                  pl.BlockSpec((per, f, c), lambda i, g: (g, 0, 0)),
                  _full(g_final.shape)],
        out_specs=row,
        scratch_shapes=[pltpu.VMEM((tm, c), BF16), pltpu.VMEM((tm, LANES), F32),
                        pltpu.VMEM((tm, c), F32)],
        compiler_params=_params("parallel", "arbitrary"),
        name="moe",
    )(h, gn, w_router, b_router, w_gate, w_up, w_down, g_final)


def _norm_proj_kernel(h_ref, gn_ref, w_ref, o_ref, *, scale):
    xn = (_rms(h_ref[...]) * gn_ref[...]).astype(BF16)
    o_ref[...] = (_dot(xn, w_ref[...]) * scale).astype(o_ref.dtype)


def _norm_proj(h, gn, w, scale):
    m, c = h.shape
    tm = TM_PROJ
    n = w.shape[1]
    return pl.pallas_call(
        functools.partial(_norm_proj_kernel, scale=scale),
        out_shape=jax.ShapeDtypeStruct((m, n), BF16),
        grid=(m // tm,),
        in_specs=[pl.BlockSpec((tm, c), lambda i: (i, 0)), _full(gn.shape), _full(w.shape)],
        out_specs=pl.BlockSpec((tm, n), lambda i: (i, 0)),
        compiler_params=_params("parallel"),
        name="norm_proj",
    )(h, gn, w)


MASK_BIAS = -1e30
SOFTPLUS_LINEAR = 100.0
ATTN_SETTLED = 160.0


def _sb_attn_kernel(q_ref, k_ref, v_ref, o_ref, carry_ref, acc_ref, bias_ref, qh_ref, vh_ref,
                    *slot_refs, tq, nq):
    z_ref, e_ref, tot_ref, att_ref = zip(slot_refs[:4], slot_refs[4:])
    tk = tq
    heads = range(HEADS_PER_BLOCK)
    lane = lax.broadcasted_iota(jnp.int32, (1, LANES), 1)
    head0 = lane < HEAD_DIM
    jr = lax.broadcasted_iota(jnp.int32, (tk, tk), 0)
    sc = lax.broadcasted_iota(jnp.int32, (tk, tk), 1)
    suffix_mat = (jr > sc).astype(BF16)
    ones_mat = jnp.ones((tk, LANES), BF16)
    zero = jnp.zeros((), BF16)

    def rows(ref, idx, n):
        return ref[pl.ds(pl.multiple_of(idx * n, n), n), :]

    for ref in slot_refs + (carry_ref, acc_ref):
        ref[...] = jnp.zeros_like(ref)
    bias_ref[0] = jnp.zeros((tq, tk), F32)
    bias_ref[1] = jnp.where(sc < jr, 0.0, MASK_BIAS)
    q = q_ref[0]
    v = v_ref[0]
    seq = nq * tq
    qh_ref[0] = jnp.where(head0, q, zero)
    qh_ref[1] = jnp.where(head0, zero, q)
    vh_ref[0, :seq] = jnp.where(head0, v, zero)
    vh_ref[1, :seq] = jnp.where(head0, zero, v)
    vh_ref[:, seq:] = jnp.zeros((HEADS_PER_BLOCK, tk, LANES), BF16)

    def stage0(t, kv, slot):
        kb = rows(k_ref.at[0], jnp.minimum(kv, nq - 1), tk)
        bias = bias_ref[(kv == t).astype(jnp.int32)]
        for h in heads:
            z_ref[slot][h] = _dot_nt(rows(qh_ref.at[h], t, tq), kb) + bias

    def stage1_head(slot):
        sp16, zs = [], []
        for h in heads:
            z = z_ref[slot][h]
            sp = jnp.maximum(z, jnp.log(1.0 + jnp.exp2(jnp.minimum(z, SOFTPLUS_LINEAR))) * LOG2E)
            sp16.append(sp.astype(BF16))
            zs.append(z - sp)
        return sp16, zs

    def stage1_tail(slot, sp16, zs):
        for h in heads:
            e_ref[slot][h] = zs[h] - _dot(sp16[h], suffix_mat)
            tot_ref[slot][h] = _dot(sp16[h], ones_mat)

    def stage2(t, kv, slot):
        first = kv == t
        for h in heads:
            carry = jnp.where(first, 0.0, carry_ref[h])
            att = jnp.exp2(e_ref[slot][h] - jnp.concatenate([carry] * (tk // LANES), axis=1))
            att_ref[slot][h] = att.astype(BF16)
            carry_ref[h] = carry + tot_ref[slot][h]

    def stage3(t, kv, slot):
        pv = (_dot(att_ref[slot][0], rows(vh_ref.at[0], kv, tk))
              + _dot(att_ref[slot][1], rows(vh_ref.at[1], kv, tk)))
        acc = jnp.where(kv == t, 0.0, acc_ref[...]) + pv
        acc_ref[...] = acc
        o_ref[0, pl.ds(pl.multiple_of(t * tq, tq), tq), :] = acc.astype(o_ref.dtype)

    def advance(t0, k0, t3, settled):
        ended = k0 == nq
        tile_done = (k0 == 0) | (settled & (t0 == t3))
        nt = jnp.where(tile_done, t0 + 1, t0)
        ended = ended | (nt == nq)
        return (jnp.where(ended, nq - 1, nt),
                jnp.where(ended, nq, jnp.where(tile_done, nt, k0 - 1)))

    def step(j, state):
        ((t0, k0), (t1, k1), (t2, k2), (t3, k3)), n_ended = state
        cur = j % 2
        old = 1 - cur
        settled = jnp.min(carry_ref[...]) >= ATTN_SETTLED
        stage0(t0, k0, cur)
        stage3(t3, k3, old)
        held = stage1_head(old)
        stage2(t2, k2, cur)
        stage1_tail(old, *held)
        n_ended = n_ended + (k0 == nq).astype(jnp.int32)
        return (advance(t0, k0, t3, settled), (t0, k0), (t1, k1), (t2, k2)), n_ended

    def trip(state):
        for j in range(ATTN_STEPS_PER_TRIP):
            state = step(j, state)
        return state

    z32 = jnp.int32(0)
    lax.while_loop(lambda state: state[1] < 3, trip, (((z32, z32),) * 4, z32))


def _sb_attn(q, kv):
    bsz, seq, c = q.shape
    tq = ATTN_TILE
    assert seq % tq == 0
    nhb = c // LANES
    qspec = pl.BlockSpec((1, seq, LANES), lambda b, h: (b, 0, h))
    vspec = pl.BlockSpec((1, seq, LANES), lambda b, h: (b, 0, nhb + h))
    blk = (HEADS_PER_BLOCK, tq, tq)
    slot = [pltpu.VMEM(blk, F32), pltpu.VMEM(blk, F32),
            pltpu.VMEM((HEADS_PER_BLOCK, tq, LANES), F32), pltpu.VMEM(blk, BF16)]
    return pl.pallas_call(
        functools.partial(_sb_attn_kernel, tq=tq, nq=seq // tq),
        out_shape=jax.ShapeDtypeStruct((bsz, seq, c), BF16),
        grid=(bsz, nhb),
        in_specs=[qspec, qspec, vspec],
        out_specs=qspec,
        scratch_shapes=[pltpu.VMEM((HEADS_PER_BLOCK, tq, LANES), F32), pltpu.VMEM((tq, LANES), F32),
                        pltpu.VMEM((2, tq, tq), F32),
                        pltpu.VMEM((HEADS_PER_BLOCK, seq, LANES), BF16),
                        pltpu.VMEM((HEADS_PER_BLOCK, seq + tq, LANES), BF16)] + slot + slot,
        compiler_params=_params("parallel", "parallel"),
        name="sb_attn",
    )(q, kv, kv)


def _router_params(w_group, b_group, w_inner, b_inner):
    c = w_group.shape[0]
    w = jnp.concatenate([w_group, w_inner.reshape(c, N_EXPERTS)], axis=1)
    b = jnp.concatenate([b_group, b_inner.reshape(N_EXPERTS)])
    pad = LANES - w.shape[1]
    return jnp.pad(w, ((0, 0), (0, pad))), jnp.pad(b, (0, pad)).reshape(1, LANES)


def kernel(x, norm_mix, norm_ffn, norm_kv, norm_final, rw_mix, rw_w_rkv, rw_w0, rw_w1, rw_w2, rw_a0, rw_a1, rw_a2, rw_g1, rw_g2, rw_k_k, rw_k_a, rw_r_k, rw_lnx_w, rw_lnx_b, rw_w_out, w_kv, sb_w_q, sb_w_out, moe_w_group, moe_b_group, moe_w_inner, moe_b_inner, moe_w_gate, moe_w_up, moe_w_down):
    bsz, seq, c = x.shape
    depth = norm_mix.shape[0]
    n_a = rw_mix.shape[0]
    m = bsz * seq
    vec = lambda t: t.reshape(1, c)
    bf = lambda t: t.astype(BF16)

    h = x.reshape(m, c)
    kv_sh = None
    for l in range(depth):
        if l < n_a:
            i = l
            r, k, v, lw, kk, a, g = _rwkv_pre(
                h, seq, vec(norm_mix[l]), jnp.pad(rw_mix[i], ((0, 2), (0, 0))), bf(rw_w_rkv[i]),
                vec(rw_w0[i]), bf(rw_w1[i]), bf(rw_w2[i]), vec(rw_a0[i]), bf(rw_a1[i]), bf(rw_a2[i]),
                bf(rw_g1[i]), bf(rw_g2[i]), vec(rw_k_k[i]), vec(rw_k_a[i]))
            s3 = lambda t: t.reshape(bsz, seq, c)
            y = _wkv(s3(r), s3(lw), s3(k), s3(v), s3(kk), s3(a), vec(rw_r_k[i]),
                     vec(rw_lnx_w[i]), vec(rw_lnx_b[i]))
            h = _proj_res(y.reshape(m, c), g, bf(rw_w_out[i]), h)
        else:
            j = l - n_a
            q = _norm_proj(h, vec(norm_mix[l]), bf(sb_w_q[j]), LOG2E * HEAD_DIM ** -0.5)
            o = _sb_attn(q.reshape(bsz, seq, c), kv_sh)
            h = _proj_res(o.reshape(m, c), None, bf(sb_w_out[j]), h)
        w_r, b_r = _router_params(moe_w_group[l], moe_b_group[l], moe_w_inner[l], moe_b_inner[l])
        h = _moe(h, vec(norm_ffn[l]), w_r, b_r, bf(moe_w_gate[l]), bf(moe_w_up[l]), bf(moe_w_down[l]),
                 vec(norm_final), l == depth - 1)
        if l == n_a - 1:
            kv_sh = _norm_proj(h, vec(norm_kv), bf(w_kv), 1.0).reshape(bsz, seq, 2 * c)
    return h.reshape(bsz, seq, c)
```

```python
import functools

import jax
import jax.numpy as jnp
from jax import lax
from jax.experimental import pallas as pl
from jax.experimental.pallas import tpu as pltpu

F32 = jnp.float32
BF16 = jnp.bfloat16

LANES = 128
HEAD_DIM = 64
HEADS_PER_BLOCK = LANES // HEAD_DIM
RMS_EPS = 1e-6
GN_EPS = 64e-5
N_GROUPS = 4
EXPERTS_PER_GROUP = 4
N_EXPERTS = N_GROUPS * EXPERTS_PER_GROUP
WKV_CHUNK = 64
VMEM_LIMIT = 56 * 1024 * 1024
LOG2E = 1.4426950408889634

TM_RWKV_PRE = 256
TM_PROJ = 512
TM_MOE = 1024
WKV_CHUNKS_PER_STEP = 4
WKV_PAIRS_PER_STEP = 4
ATTN_TILE = 2 * LANES


def _params(*sem):
    return pltpu.CompilerParams(dimension_semantics=sem, vmem_limit_bytes=VMEM_LIMIT)


def _dot(a, b):
    return lax.dot_general(a, b, (((1,), (0,)), ((), ())), preferred_element_type=F32)


def _dot_nt(a, b):
    return lax.dot_general(a, b, (((1,), (1,)), ((), ())), preferred_element_type=F32)


def _dot_tn(a, b):
    return lax.dot_general(a, b, (((0,), (0,)), ((), ())), preferred_element_type=F32)


def _bf16_terms(x, n):
    terms = []
    for _ in range(n):
        t = x.astype(BF16)
        terms.append(t)
        x = x - t.astype(F32)
    return terms


def _dot_mask_rhs(x, mask01, n):
    m = mask01.astype(BF16)
    return sum(_dot(t, m) for t in _bf16_terms(x, n))


def _dot_mask_lhs(mask01, x, n):
    m = mask01.astype(BF16)
    return sum(_dot(m, t) for t in _bf16_terms(x, n))


def _dot_3pass(a, b):
    a_hi, a_lo = _bf16_terms(a, 2)
    b_hi, b_lo = _bf16_terms(b, 2)
    return _dot(a_hi, b_hi) + _dot(a_hi, b_lo) + _dot(a_lo, b_hi)


def _rms(x):
    return x * lax.rsqrt(jnp.mean(x * x, axis=-1, keepdims=True) + RMS_EPS)


def _sigmoid(x):
    return 1.0 / (1.0 + jnp.exp(-x))


def _softplus(x):
    return jnp.maximum(x, 0.0) + jnp.log(1.0 + jnp.exp(-jnp.abs(x)))


def _full(shape):
    nd = len(shape)
    return pl.BlockSpec(shape, lambda *_: (0,) * nd)


def _rwkv_pre_kernel(h_ref, hp_ref, gn_ref, mix_ref, wrkv_ref, w0_ref, w1_ref, w2_ref,
                     a0_ref, a1_ref, a2_ref, g1_ref, g2_ref, kk_ref, ka_ref,
                     r_out, k_out, v_out, lw_out, kk_out, a_out, g_out, *, tiles_per_seq):
    i = pl.program_id(0)
    gn = gn_ref[...]
    hn = _rms(h_ref[...]) * gn
    hpn = _rms(hp_ref[7:8, :]) * gn
    hpn = jnp.where(i % tiles_per_seq == 0, 0.0, hpn)
    row = lax.broadcasted_iota(jnp.int32, hn.shape, 0)
    prev = jnp.where(row == 0, hpn, pltpu.roll(hn, 1, axis=0))
    xx = prev - hn

    def mixed(n):
        return (hn + xx * mix_ref[n:n + 1, :]).astype(BF16)

    r = _dot(mixed(0), wrkv_ref[0])
    k = _dot(mixed(1), wrkv_ref[1])
    v = _dot(mixed(2), wrkv_ref[2])
    wl = w0_ref[...] + _dot(jnp.tanh(_dot(mixed(3), w1_ref[...])).astype(BF16), w2_ref[...])
    a = _sigmoid(a0_ref[...] + _dot(_dot(mixed(4), a1_ref[...]).astype(BF16), a2_ref[...]))
    g = _dot(_sigmoid(_dot(mixed(5), g1_ref[...])).astype(BF16), g2_ref[...])
    r_out[...] = r
    k_out[...] = k * (1.0 + (a - 1.0) * ka_ref[...])
    v_out[...] = v
    lw_out[...] = -jnp.exp(-_softplus(-wl) - 0.5)
    kk_out[...] = k * kk_ref[...]
    a_out[...] = a
    g_out[...] = g


def _rwkv_pre(h, seq, gn, mix, wrkv, w0, w1, w2, a0, a1, a2, g1, g2, k_k, k_a):
    m, c = h.shape
    tm = TM_RWKV_PRE
    assert seq % tm == 0 and m % seq == 0
    row = lambda i: (i, 0)
    out = jax.ShapeDtypeStruct((m, c), F32)
    ins = [gn, mix, wrkv, w0, w1, w2, a0, a1, a2, g1, g2, k_k, k_a]
    return pl.pallas_call(
        functools.partial(_rwkv_pre_kernel, tiles_per_seq=seq // tm),
        out_shape=[out] * 7,
        grid=(m // tm,),
        in_specs=[pl.BlockSpec((tm, c), row),
                  pl.BlockSpec((8, c), lambda i: (jnp.maximum(i * (tm // 8) - 1, 0), 0))]
                 + [_full(t.shape) for t in ins],
        out_specs=[pl.BlockSpec((tm, c), row)] * 7,
        compiler_params=_params("parallel"),
        name="rwkv_pre",
    )(h, h, *ins)


def _wkv_kernel(r_ref, lw_ref, k_ref, v_ref, kk_ref, a_ref, rk_ref, gw_ref, gb_ref,
                y_ref, st_ref, *, chunks, pairs):
    L = WKV_CHUNK

    @pl.when(pl.program_id(2) == 0)
    def _():
        st_ref[...] = jnp.zeros_like(st_ref)

    tl = chunks * L
    lane = lax.broadcasted_iota(jnp.int32, (1, LANES), 1)
    head0 = lane < HEAD_DIM
    rr = lax.broadcasted_iota(jnp.int32, (LANES, LANES), 0)
    cc = lax.broadcasted_iota(jnp.int32, (LANES, LANES), 1)
    bd = ((rr < HEAD_DIM) == (cc < HEAD_DIM)).astype(F32)
    eye = rr == cc
    tr = lax.broadcasted_iota(jnp.int32, (tl, tl), 0)
    tc = lax.broadcasted_iota(jnp.int32, (tl, tl), 1)
    shift = L.bit_length() - 1
    same_chunk = (tr >> shift) == (tc >> shift)
    strict = same_chunk & (tr > tc)
    incl = same_chunk & (tr >= tc)
    tri = incl.astype(F32)
    eye_t = (tr == tc).astype(F32)
    m0 = head0.astype(F32)
    masks = (m0, 1.0 - m0)
    head0w = jnp.concatenate([head0, head0], axis=1)
    sel = lambda x0, x1: jnp.where(head0 if x0.shape[1] == LANES else head0w, x0, x1)

    prs = range(pairs)
    heads = range(HEADS_PER_BLOCK)
    units = [(p, h) for p in prs for h in heads]
    lanes = [slice(p * LANES, (p + 1) * LANES) for p in prs]

    r, v, k, at, at16, bt16, kt16, rt, bh16, kh16, v16, cum_end = ([] for _ in range(12))
    for p in prs:
        r_p = r_ref[0, :, lanes[p]]
        lw = lw_ref[0, :, lanes[p]]
        k_p = k_ref[0, :, lanes[p]]
        v_p = v_ref[0, :, lanes[p]]
        kkr = kk_ref[0, :, lanes[p]]
        ag = a_ref[0, :, lanes[p]]
        cum = _dot_mask_lhs(tri, lw, 3)
        ends = [cum[(c + 1) * L - 1:(c + 1) * L, :] for c in range(chunks)]
        cum_l = jnp.concatenate([jnp.broadcast_to(e, (L, LANES)) for e in ends], axis=0)
        w_inv = jnp.exp(-cum)
        w_tail = jnp.exp(cum_l - cum)
        kk = kkr / jnp.maximum(jnp.sqrt(_dot_mask_rhs(kkr * kkr, bd, 1)), 1e-12)
        b = kk * ag
        at_p = -kk * jnp.exp(cum - lw)
        r.append(r_p)
        k.append(k_p)
        v.append(v_p)
        cum_end.append(ends)
        at.append(at_p)
        at16.append(at_p.astype(BF16))
        bt16.append((b * w_inv).astype(BF16))
        kt16.append((k_p * w_inv).astype(BF16))
        rt.append(r_p * jnp.exp(cum))
        bh16.append((b * w_tail).astype(BF16))
        kh16.append((k_p * w_tail).astype(BF16))
        v16.append(v_p.astype(BF16))

    a_ab, a_ak, a_rb, a_rk = [], [], [], []
    for p, h in units:
        lhs = jnp.concatenate([at[p] * masks[h], rt[p] * masks[h]], axis=0).astype(BF16)
        pb = _dot_nt(lhs, bt16[p])
        pk = _dot_nt(lhs, kt16[p])
        a_ab.append(jnp.where(strict, pb[:tl], 0.0))
        a_ak.append(jnp.where(strict, pk[:tl], 0.0).astype(BF16))
        a_rb.append(jnp.where(incl, pb[tl:], 0.0).astype(BF16))
        a_rk.append(jnp.where(incl, pk[tl:], 0.0).astype(BF16))
    n_units = range(len(units))
    t = [eye_t + a for a in a_ab]
    pw = [a.astype(BF16) for a in a_ab]
    akv = [_dot(a_ak[u], v16[units[u][0]]).astype(BF16) for u in n_units]
    rkv = [_dot(a_rk[u], v16[units[u][0]]) for u in n_units]
    for _ in range(L.bit_length() - 2):
        pw = [_dot(pw[u], pw[u]).astype(BF16) for u in n_units]
        t = [t[u] + _dot(t[u].astype(BF16), pw[u]) for u in n_units]
    tau = [_dot(t[u].astype(BF16), jnp.concatenate([at16[units[u][0]], akv[u]], axis=1)).astype(BF16)
           for u in n_units]
    rpy = [_dot(a_rb[u], tau[u]) for u in n_units]

    ta16, u16, rp16, yi = [], [], [], []
    for p in prs:
        u0 = p * HEADS_PER_BLOCK
        tau_p = sel(tau[u0], tau[u0 + 1])
        rpy_p = sel(rpy[u0], rpy[u0 + 1])
        ta16.append(tau_p[:, :LANES])
        u16.append(tau_p[:, LANES:])
        rp16.append((rt[p] + rpy_p[:, :LANES]).astype(BF16))
        yi.append(rpy_p[:, LANES:] + sel(rkv[u0], rkv[u0 + 1]))

    mts, gts = [], []
    for c in range(chunks):
        rows = slice(c * L, (c + 1) * L)
        mts.append([(jnp.where(eye, jnp.exp(cum_end[p][c]), 0.0)
                     + bd * _dot_tn(bh16[p][rows], ta16[p][rows])).astype(BF16) for p in prs])
        gts.append([bd * (_dot_tn(bh16[p][rows], u16[p][rows]) + _dot_tn(kh16[p][rows], v16[p][rows]))
                    for p in prs])
    st = [st_ref[p] for p in prs]
    ys = [[] for _ in prs]
    for c in range(chunks):
        rows = slice(c * L, (c + 1) * L)
        st16 = [s.astype(BF16) for s in st]
        for p in prs:
            ys[p].append(_dot(rp16[p][rows], st16[p]) + yi[p][rows])
        st = [_dot(mts[c][p], st16[p]) + gts[c][p] for p in prs]
    for p in prs:
        st_ref[p] = st[p]
        y = jnp.concatenate(ys[p], axis=0)
        mu = _dot_mask_rhs(y, bd, 1) * (1.0 / HEAD_DIM)
        d = y - mu
        var = _dot_mask_rhs(d * d, bd, 1) * (1.0 / HEAD_DIM)
        yn = d * lax.rsqrt(var + GN_EPS) * gw_ref[:, lanes[p]] + gb_ref[:, lanes[p]]
        bonus = _dot_mask_rhs(r[p] * k[p] * rk_ref[:, lanes[p]], bd, 1) * v[p]
        y_ref[0, :, lanes[p]] = yn + bonus


def _wkv(r, lw, k, v, kk, a, r_k, lnx_w, lnx_b):
    bsz, seq, c = r.shape
    chunks, pairs = WKV_CHUNKS_PER_STEP, WKV_PAIRS_PER_STEP
    tl = chunks * WKV_CHUNK
    width = pairs * LANES
    assert seq % tl == 0 and c % width == 0
    tok = pl.BlockSpec((1, tl, width), lambda b, h, t: (b, t, h))
    par = pl.BlockSpec((1, width), lambda b, h, t: (0, h))
    return pl.pallas_call(
        functools.partial(_wkv_kernel, chunks=chunks, pairs=pairs),
        out_shape=jax.ShapeDtypeStruct((bsz, seq, c), F32),
        grid=(bsz, c // width, seq // tl),
        in_specs=[tok] * 6 + [par] * 3,
        out_specs=tok,
        scratch_shapes=[pltpu.VMEM((pairs, LANES, LANES), F32)],
        compiler_params=_params("parallel", "parallel", "arbitrary"),
        name="wkv",
    )(r, lw, k, v, kk, a, r_k, lnx_w, lnx_b)


def _proj_res_kernel(*refs, gated):
    if gated:
        x_ref, g_ref, w_ref, res_ref, o_ref = refs
        x = (x_ref[...] * g_ref[...]).astype(BF16)
    else:
        x_ref, w_ref, res_ref, o_ref = refs
        x = x_ref[...]
    o_ref[...] = res_ref[...] + _dot(x, w_ref[...])


def _proj_res(x, gate, w, res):
    m, c = res.shape
    tm = TM_PROJ
    row = pl.BlockSpec((tm, c), lambda i: (i, 0))
    gated = gate is not None
    ins = [x, gate, w, res] if gated else [x, w, res]
    specs = [row, row, _full(w.shape), row] if gated else [row, _full(w.shape), row]
    return pl.pallas_call(
        functools.partial(_proj_res_kernel, gated=gated),
        out_shape=jax.ShapeDtypeStruct((m, c), F32),
        grid=(m // tm,),
        in_specs=specs,
        out_specs=row,
        compiler_params=_params("parallel"),
        name="proj_res",
    )(*ins)


def _moe_kernel(h_ref, gn_ref, wr_ref, br_ref, wg_ref, wu_ref, wd_ref, gf_ref, o_ref,
                xn_ref, gates_ref, acc_ref, *, final_norm, d_expert):
    g = pl.program_id(1)
    lane = lax.broadcasted_iota(jnp.int32, (1, LANES), 1)

    @pl.when(g == 0)
    def _():
        xn = _rms(h_ref[...]) * gn_ref[...]
        xn_ref[...] = xn.astype(BF16)
        logits = _dot_3pass(xn, wr_ref[...]) + br_ref[...]
        neg = jnp.float32(-jnp.inf)
        is_g = lane < N_GROUPS
        gl = jnp.where(is_g, logits, neg)
        gmax = jnp.max(gl, axis=-1, keepdims=True)
        gval = 1.0 / jnp.sum(jnp.exp(gl - gmax), axis=-1, keepdims=True)
        gidx = jnp.min(jnp.where(gl == gmax, lane, LANES), axis=-1, keepdims=True)
        ex = lane - N_GROUPS
        in_grp = (ex >= gidx * EXPERTS_PER_GROUP) & (ex < (gidx + 1) * EXPERTS_PER_GROUP)
        il = jnp.where(in_grp, logits, neg)
        l1 = jnp.max(il, axis=-1, keepdims=True)
        i1 = jnp.min(jnp.where(il == l1, lane, LANES), axis=-1, keepdims=True)
        il2 = jnp.where(lane == i1, neg, il)
        l2 = jnp.max(il2, axis=-1, keepdims=True)
        i2 = jnp.min(jnp.where(il2 == l2, lane, LANES), axis=-1, keepdims=True)
        e2 = jnp.exp(l2 - l1)
        w1 = gval / (1.0 + e2)
        w2 = gval * e2 / (1.0 + e2)
        gates_ref[...] = jnp.where(lane == i1, w1, 0.0) + jnp.where(lane == i2, w2, 0.0)
        acc_ref[...] = jnp.zeros_like(acc_ref)

    xn = xn_ref[...]
    gates = gates_ref[...]
    first = N_GROUPS + g * EXPERTS_PER_GROUP
    hm = []
    for e in range(EXPERTS_PER_GROUP):
        hg = _dot(xn, wg_ref[e])
        hu = _dot(xn, wu_ref[e])
        ge = jnp.sum(jnp.where(lane == first + e, gates, 0.0), axis=-1, keepdims=True)
        hm.append(((hg * _sigmoid(hg)) * hu * ge).astype(BF16))
    wd = wd_ref[...].reshape(EXPERTS_PER_GROUP * d_expert, wd_ref.shape[-1])
    acc_ref[...] += _dot(jnp.concatenate(hm, axis=1), wd)

    @pl.when(g == N_GROUPS - 1)
    def _():
        out = h_ref[...] + acc_ref[...]
        if final_norm:
            out = _rms(out) * gf_ref[...]
        o_ref[...] = out


def _moe(h, gn, w_router, b_router, w_gate, w_up, w_down, g_final, final_norm):
    m, c = h.shape
    tm = TM_MOE
    f = w_gate.shape[-1]
    per = EXPERTS_PER_GROUP
    row = pl.BlockSpec((tm, c), lambda i, g: (i, 0))
    return pl.pallas_call(
        functools.partial(_moe_kernel, final_norm=final_norm, d_expert=f),
        out_shape=jax.ShapeDtypeStruct((m, c), F32),
        grid=(m // tm, N_GROUPS),
        in_specs=[row, _full(gn.shape), _full(w_router.shape), _full(b_router.shape),
                  pl.BlockSpec((per, c, f), lambda i, g: (g, 0, 0)),
                  pl.BlockSpec((per, c, f), lambda i, g: (g, 0, 0)),
                  pl.BlockSpec((per, f, c), lambda i, g: (g, 0, 0)),
                  _full(g_final.shape)],
        out_specs=row,
        scratch_shapes=[pltpu.VMEM((tm, c), BF16), pltpu.VMEM((tm, LANES), F32),
                        pltpu.VMEM((tm, c), F32)],
        compiler_params=_params("parallel", "arbitrary"),
        name="moe",
    )(h, gn, w_router, b_router, w_gate, w_up, w_down, g_final)


def _norm_proj_kernel(h_ref, gn_ref, w_ref, o_ref, *, scale):
    xn = (_rms(h_ref[...]) * gn_ref[...]).astype(BF16)
    o_ref[...] = (_dot(xn, w_ref[...]) * scale).astype(o_ref.dtype)


def _norm_proj(h, gn, w, scale):
    m, c = h.shape
    tm = TM_PROJ
    n = w.shape[1]
    return pl.pallas_call(
        functools.partial(_norm_proj_kernel, scale=scale),
        out_shape=jax.ShapeDtypeStruct((m, n), BF16),
        grid=(m // tm,),
        in_specs=[pl.BlockSpec((tm, c), lambda i: (i, 0)), _full(gn.shape), _full(w.shape)],
        out_specs=pl.BlockSpec((tm, n), lambda i: (i, 0)),
        compiler_params=_params("parallel"),
        name="norm_proj",
    )(h, gn, w)


MASK_BIAS = -1e30
SOFTPLUS_LINEAR = 100.0
ATTN_SETTLED = 160.0


def _sb_attn_kernel(q_ref, k_ref, v_ref, o_ref, carry_ref, acc_ref, bias_ref, qh_ref, vh_ref,
                    *slot_refs, tq, nq):
    z_ref, e_ref, tot_ref, att_ref = zip(slot_refs[:4], slot_refs[4:])
    tk = tq
    heads = range(HEADS_PER_BLOCK)
    lane = lax.broadcasted_iota(jnp.int32, (1, LANES), 1)
    head0 = lane < HEAD_DIM
    jr = lax.broadcasted_iota(jnp.int32, (tk, tk), 0)
    sc = lax.broadcasted_iota(jnp.int32, (tk, tk), 1)
    suffix_mat = (jr > sc).astype(BF16)
    ones_mat = jnp.ones((tk, LANES), BF16)
    zero = jnp.zeros((), BF16)

    def rows(ref, idx, n):
        return ref[pl.ds(pl.multiple_of(idx * n, n), n), :]

    for ref in slot_refs + (carry_ref, acc_ref):
        ref[...] = jnp.zeros_like(ref)
    bias_ref[0] = jnp.zeros((tq, tk), F32)
    bias_ref[1] = jnp.where(sc < jr, 0.0, MASK_BIAS)
    q = q_ref[0]
    v = v_ref[0]
    seq = nq * tq
    qh_ref[0] = jnp.where(head0, q, zero)
    qh_ref[1] = jnp.where(head0, zero, q)
    vh_ref[0, :seq] = jnp.where(head0, v, zero)
    vh_ref[1, :seq] = jnp.where(head0, zero, v)
    vh_ref[:, seq:] = jnp.zeros((HEADS_PER_BLOCK, tk, LANES), BF16)

    def stage0(t, kv, slot):
        kb = rows(k_ref.at[0], jnp.minimum(kv, nq - 1), tk)
        bias = bias_ref[(kv == t).astype(jnp.int32)]
        for h in heads:
            z_ref[slot][h] = _dot_nt(rows(qh_ref.at[h], t, tq), kb) + bias

    def stage1_head(slot):
        sp16, zs = [], []
        for h in heads:
            z = z_ref[slot][h]
            sp = jnp.maximum(z, jnp.log(1.0 + jnp.exp2(jnp.minimum(z, SOFTPLUS_LINEAR))) * LOG2E)
            sp16.append(sp.astype(BF16))
            zs.append(z - sp)
        return sp16, zs

    def stage1_tail(slot, sp16, zs):
        for h in heads:
            e_ref[slot][h] = zs[h] - _dot(sp16[h], suffix_mat)
            tot_ref[slot][h] = _dot(sp16[h], ones_mat)

    def stage2(t, kv, slot):
        first = kv == t
        low = None
        for h in heads:
            carry = jnp.where(first, 0.0, carry_ref[slot, h])
            new = carry + tot_ref[slot][h]
            carry_ref[slot, h] = new
            low = new if low is None else jnp.minimum(low, new)
            att = jnp.exp2(e_ref[slot][h] - jnp.concatenate([carry] * (tk // LANES), axis=1))
            att_ref[slot][h] = att.astype(BF16)
        return (jnp.min(low) >= ATTN_SETTLED) & (kv != nq)

    def stage3(t, kv, slot):
        pv = (_dot(att_ref[slot][0], rows(vh_ref.at[0], kv, tk))
              + _dot(att_ref[slot][1], rows(vh_ref.at[1], kv, tk)))
        acc = jnp.where(kv == t, 0.0, acc_ref[slot]) + pv
        acc_ref[slot] = acc
        o_ref[0, pl.ds(pl.multiple_of(t * tq, tq), tq), :] = acc.astype(o_ref.dtype)

    def advance(t0, k0, t2, settled, t_end):
        tile_done = (k0 == 0) | (settled & (t0 == t2))
        nt = jnp.where(tile_done, t0 + 1, t0)
        ended = (k0 == nq) | (nt == t_end)
        return (jnp.where(ended, t_end - 1, nt),
                jnp.where(ended, nq, jnp.where(tile_done, nt, k0 - 1)))

    half = nq // 2
    t_end = (half, nq)

    def step(cur, state):
        nxt, (b1, b2, b3), n_ended = state
        old = 1 - cur
        t0, k0 = nxt[cur]
        stage0(t0, k0, cur)
        stage3(*b3, old)
        held = stage1_head(old)
        settled = stage2(*b2, cur)
        stage1_tail(old, *held)
        moved = advance(t0, k0, b2[0], settled, t_end[cur])
        nxt = (moved, nxt[1]) if cur == 0 else (nxt[0], moved)
        ended = (k0 == nq).astype(jnp.int32)
        n_ended = (n_ended[0] + ended, n_ended[1]) if cur == 0 else (n_ended[0], n_ended[1] + ended)
        return nxt, ((t0, k0), b1, b2), n_ended

    def trip(state):
        return step(1, step(0, state))

    i32 = lambda x: jnp.int32(x)
    idle = ((i32(half - 1), i32(nq)), (i32(nq - 1), i32(nq)))
    start = ((i32(0), i32(0)), (i32(half), i32(half)))
    lax.while_loop(lambda s: (s[2][0] < 2) | (s[2][1] < 2), trip,
                   (start, (idle[1], idle[0], idle[1]), (i32(0), i32(0))))


def _sb_attn(q, kv):
    bsz, seq, c = q.shape
    tq = ATTN_TILE
    assert seq % (2 * tq) == 0
    nhb = c // LANES
    qspec = pl.BlockSpec((1, seq, LANES), lambda b, h: (b, 0, h))
    vspec = pl.BlockSpec((1, seq, LANES), lambda b, h: (b, 0, nhb + h))
    blk = (HEADS_PER_BLOCK, tq, tq)
    slot = [pltpu.VMEM(blk, F32), pltpu.VMEM(blk, F32),
            pltpu.VMEM((HEADS_PER_BLOCK, tq, LANES), F32), pltpu.VMEM(blk, BF16)]
    return pl.pallas_call(
        functools.partial(_sb_attn_kernel, tq=tq, nq=seq // tq),
        out_shape=jax.ShapeDtypeStruct((bsz, seq, c), BF16),
        grid=(bsz, nhb),
        in_specs=[qspec, qspec, vspec],
        out_specs=qspec,
        scratch_shapes=[pltpu.VMEM((2, HEADS_PER_BLOCK, tq, LANES), F32), pltpu.VMEM((2, tq, LANES), F32),
                        pltpu.VMEM((2, tq, tq), F32),
                        pltpu.VMEM((HEADS_PER_BLOCK, seq, LANES), BF16),
                        pltpu.VMEM((HEADS_PER_BLOCK, seq + tq, LANES), BF16)] + slot + slot,
        compiler_params=_params("parallel", "parallel"),
        name="sb_attn",
    )(q, kv, kv)


def _router_params(w_group, b_group, w_inner, b_inner):
    c = w_group.shape[0]
    w = jnp.concatenate([w_group, w_inner.reshape(c, N_EXPERTS)], axis=1)
    b = jnp.concatenate([b_group, b_inner.reshape(N_EXPERTS)])
    pad = LANES - w.shape[1]
    return jnp.pad(w, ((0, 0), (0, pad))), jnp.pad(b, (0, pad)).reshape(1, LANES)


def kernel(x, norm_mix, norm_ffn, norm_kv, norm_final, rw_mix, rw_w_rkv, rw_w0, rw_w1, rw_w2, rw_a0, rw_a1, rw_a2, rw_g1, rw_g2, rw_k_k, rw_k_a, rw_r_k, rw_lnx_w, rw_lnx_b, rw_w_out, w_kv, sb_w_q, sb_w_out, moe_w_group, moe_b_group, moe_w_inner, moe_b_inner, moe_w_gate, moe_w_up, moe_w_down):
    bsz, seq, c = x.shape
    depth = norm_mix.shape[0]
    n_a = rw_mix.shape[0]
    m = bsz * seq
    vec = lambda t: t.reshape(1, c)
    bf = lambda t: t.astype(BF16)

    h = x.reshape(m, c)
    kv_sh = None
    for l in range(depth):
        if l < n_a:
            i = l
            r, k, v, lw, kk, a, g = _rwkv_pre(
                h, seq, vec(norm_mix[l]), jnp.pad(rw_mix[i], ((0, 2), (0, 0))), bf(rw_w_rkv[i]),
                vec(rw_w0[i]), bf(rw_w1[i]), bf(rw_w2[i]), vec(rw_a0[i]), bf(rw_a1[i]), bf(rw_a2[i]),
                bf(rw_g1[i]), bf(rw_g2[i]), vec(rw_k_k[i]), vec(rw_k_a[i]))
            s3 = lambda t: t.reshape(bsz, seq, c)
            y = _wkv(s3(r), s3(lw), s3(k), s3(v), s3(kk), s3(a), vec(rw_r_k[i]),
                     vec(rw_lnx_w[i]), vec(rw_lnx_b[i]))
            h = _proj_res(y.reshape(m, c), g, bf(rw_w_out[i]), h)
        else:
            j = l - n_a
            q = _norm_proj(h, vec(norm_mix[l]), bf(sb_w_q[j]), LOG2E * HEAD_DIM ** -0.5)
            o = _sb_attn(q.reshape(bsz, seq, c), kv_sh)
            h = _proj_res(o.reshape(m, c), None, bf(sb_w_out[j]), h)
        w_r, b_r = _router_params(moe_w_group[l], moe_b_group[l], moe_w_inner[l], moe_b_inner[l])
        h = _moe(h, vec(norm_ffn[l]), w_r, b_r, bf(moe_w_gate[l]), bf(moe_w_up[l]), bf(moe_w_down[l]),
                 vec(norm_final), l == depth - 1)
        if l == n_a - 1:
            kv_sh = _norm_proj(h, vec(norm_kv), bf(w_kv), 1.0).reshape(bsz, seq, 2 * c)
    return h.reshape(bsz, seq, c)
```

```python
import functools

import jax
import jax.numpy as jnp
from jax import lax
from jax.experimental import pallas as pl
from jax.experimental.pallas import tpu as pltpu

F32 = jnp.float32
BF16 = jnp.bfloat16

LANES = 128
HEAD_DIM = 64
HEADS_PER_BLOCK = LANES // HEAD_DIM
RMS_EPS = 1e-6
GN_EPS = 64e-5
N_GROUPS = 4
EXPERTS_PER_GROUP = 4
N_EXPERTS = N_GROUPS * EXPERTS_PER_GROUP
WKV_CHUNK = 64
VMEM_LIMIT = 56 * 1024 * 1024
LOG2E = 1.4426950408889634

TM_RWKV_PRE = 512
TM_PROJ = 512
TM_MOE = 1024
WKV_CHUNKS_PER_STEP = 4
WKV_PAIRS_PER_STEP = 4
ATTN_TILE = 2 * LANES


def _params(*sem):
    return pltpu.CompilerParams(dimension_semantics=sem, vmem_limit_bytes=VMEM_LIMIT)


def _dot(a, b):
    return lax.dot_general(a, b, (((1,), (0,)), ((), ())), preferred_element_type=F32)


def _dot_nt(a, b):
    return lax.dot_general(a, b, (((1,), (1,)), ((), ())), preferred_element_type=F32)


def _dot_tn(a, b):
    return lax.dot_general(a, b, (((0,), (0,)), ((), ())), preferred_element_type=F32)


def _bf16_terms(x, n):
    terms = []
    for _ in range(n):
        t = x.astype(BF16)
        terms.append(t)
        x = x - t.astype(F32)
    return terms


def _dot_mask_rhs(x, mask01, n):
    m = mask01.astype(BF16)
    return sum(_dot(t, m) for t in _bf16_terms(x, n))


def _dot_mask_lhs(mask01, x, n):
    m = mask01.astype(BF16)
    return sum(_dot(m, t) for t in _bf16_terms(x, n))


def _dot_3pass(a, b):
    a_hi, a_lo = _bf16_terms(a, 2)
    b_hi, b_lo = _bf16_terms(b, 2)
    return _dot(a_hi, b_hi) + _dot(a_hi, b_lo) + _dot(a_lo, b_hi)


def _rms(x):
    return x * lax.rsqrt(jnp.mean(x * x, axis=-1, keepdims=True) + RMS_EPS)


def _sigmoid(x):
    return 1.0 / (1.0 + jnp.exp(-x))


def _softplus(x):
    return jnp.maximum(x, 0.0) + jnp.log(1.0 + jnp.exp(-jnp.abs(x)))


def _full(shape):
    nd = len(shape)
    return pl.BlockSpec(shape, lambda *_: (0,) * nd)


def _rwkv_pre_kernel(h_ref, hp_ref, gn_ref, mix_ref, wrkv_ref, w0_ref, w1_ref, w2_ref,
                     a0_ref, a1_ref, a2_ref, g1_ref, g2_ref, kk_ref, ka_ref,
                     r_out, k_out, v_out, lw_out, kk_out, a_out, g_out, *, tiles_per_seq):
    i = pl.program_id(0)
    gn = gn_ref[...]
    hn = _rms(h_ref[...]) * gn
    hpn = _rms(hp_ref[7:8, :]) * gn
    hpn = jnp.where(i % tiles_per_seq == 0, 0.0, hpn)
    row = lax.broadcasted_iota(jnp.int32, hn.shape, 0)
    prev = jnp.where(row == 0, hpn, pltpu.roll(hn, 1, axis=0))
    xx = prev - hn

    def mixed(n):
        return (hn + xx * mix_ref[n:n + 1, :]).astype(BF16)

    r = _dot(mixed(0), wrkv_ref[0])
    k = _dot(mixed(1), wrkv_ref[1])
    v = _dot(mixed(2), wrkv_ref[2])
    wl = w0_ref[...] + _dot(jnp.tanh(_dot(mixed(3), w1_ref[...])).astype(BF16), w2_ref[...])
    a = _sigmoid(a0_ref[...] + _dot(_dot(mixed(4), a1_ref[...]).astype(BF16), a2_ref[...]))
    g = _dot(_sigmoid(_dot(mixed(5), g1_ref[...])).astype(BF16), g2_ref[...])
    r_out[...] = r
    k_out[...] = k * (1.0 + (a - 1.0) * ka_ref[...])
    v_out[...] = v
    lw_out[...] = -jnp.exp(-_softplus(-wl) - 0.5)
    kk_out[...] = k * kk_ref[...]
    a_out[...] = a
    g_out[...] = g


def _rwkv_pre(h, seq, gn, mix, wrkv, w0, w1, w2, a0, a1, a2, g1, g2, k_k, k_a):
    m, c = h.shape
    tm = TM_RWKV_PRE
    assert seq % tm == 0 and m % seq == 0
    row = lambda i: (i, 0)
    out = jax.ShapeDtypeStruct((m, c), F32)
    ins = [gn, mix, wrkv, w0, w1, w2, a0, a1, a2, g1, g2, k_k, k_a]
    return pl.pallas_call(
        functools.partial(_rwkv_pre_kernel, tiles_per_seq=seq // tm),
        out_shape=[out] * 7,
        grid=(m // tm,),
        in_specs=[pl.BlockSpec((tm, c), row),
                  pl.BlockSpec((8, c), lambda i: (jnp.maximum(i * (tm // 8) - 1, 0), 0))]
                 + [_full(t.shape) for t in ins],
        out_specs=[pl.BlockSpec((tm, c), row)] * 7,
        compiler_params=_params("parallel"),
        name="rwkv_pre",
    )(h, h, *ins)


def _wkv_kernel(r_ref, lw_ref, k_ref, v_ref, kk_ref, a_ref, rk_ref, gw_ref, gb_ref,
                y_ref, st_ref, *, chunks, pairs):
    L = WKV_CHUNK

    @pl.when(pl.program_id(2) == 0)
    def _():
        st_ref[...] = jnp.zeros_like(st_ref)

    tl = chunks * L
    lane = lax.broadcasted_iota(jnp.int32, (1, LANES), 1)
    head0 = lane < HEAD_DIM
    rr = lax.broadcasted_iota(jnp.int32, (LANES, LANES), 0)
    cc = lax.broadcasted_iota(jnp.int32, (LANES, LANES), 1)
    bd = ((rr < HEAD_DIM) == (cc < HEAD_DIM)).astype(F32)
    eye = rr == cc
    tr = lax.broadcasted_iota(jnp.int32, (tl, tl), 0)
    tc = lax.broadcasted_iota(jnp.int32, (tl, tl), 1)
    shift = L.bit_length() - 1
    same_chunk = (tr >> shift) == (tc >> shift)
    strict = same_chunk & (tr > tc)
    incl = same_chunk & (tr >= tc)
    tri = incl.astype(F32)
    eye_t = (tr == tc).astype(F32)
    m0 = head0.astype(F32)
    masks = (m0, 1.0 - m0)
    head0w = jnp.concatenate([head0, head0], axis=1)
    sel = lambda x0, x1: jnp.where(head0 if x0.shape[1] == LANES else head0w, x0, x1)

    prs = range(pairs)
    heads = range(HEADS_PER_BLOCK)
    units = [(p, h) for p in prs for h in heads]
    lanes = [slice(p * LANES, (p + 1) * LANES) for p in prs]

    r, v, k, at, at16, bt16, kt16, rt, bh16, kh16, v16, cum_end = ([] for _ in range(12))
    for p in prs:
        r_p = r_ref[0, :, lanes[p]]
        lw = lw_ref[0, :, lanes[p]]
        k_p = k_ref[0, :, lanes[p]]
        v_p = v_ref[0, :, lanes[p]]
        kkr = kk_ref[0, :, lanes[p]]
        ag = a_ref[0, :, lanes[p]]
        cum = _dot_mask_lhs(tri, lw, 3)
        ends = [cum[(c + 1) * L - 1:(c + 1) * L, :] for c in range(chunks)]
        cum_l = jnp.concatenate([jnp.broadcast_to(e, (L, LANES)) for e in ends], axis=0)
        w_inv = jnp.exp(-cum)
        w_tail = jnp.exp(cum_l - cum)
        kk = kkr / jnp.maximum(jnp.sqrt(_dot_mask_rhs(kkr * kkr, bd, 1)), 1e-12)
        b = kk * ag
        at_p = -kk * jnp.exp(cum - lw)
        r.append(r_p)
        k.append(k_p)
        v.append(v_p)
        cum_end.append(ends)
        at.append(at_p)
        at16.append(at_p.astype(BF16))
        bt16.append((b * w_inv).astype(BF16))
        kt16.append((k_p * w_inv).astype(BF16))
        rt.append(r_p * jnp.exp(cum))
        bh16.append((b * w_tail).astype(BF16))
        kh16.append((k_p * w_tail).astype(BF16))
        v16.append(v_p.astype(BF16))

    a_ab, a_ak, a_rb, a_rk = [], [], [], []
    for p, h in units:
        lhs = jnp.concatenate([at[p] * masks[h], rt[p] * masks[h]], axis=0).astype(BF16)
        pb = _dot_nt(lhs, bt16[p])
        pk = _dot_nt(lhs, kt16[p])
        a_ab.append(jnp.where(strict, pb[:tl], 0.0))
        a_ak.append(jnp.where(strict, pk[:tl], 0.0).astype(BF16))
        a_rb.append(jnp.where(incl, pb[tl:], 0.0).astype(BF16))
        a_rk.append(jnp.where(incl, pk[tl:], 0.0).astype(BF16))
    def side_by_side(x):
        return sum(x[c * L:(c + 1) * L] for c in range(chunks))

    def block_diag(x16):
        return jnp.where(same_chunk, jnp.concatenate([x16] * chunks, axis=0), jnp.zeros((), BF16))

    n_units = range(len(units))
    eye_w = side_by_side(eye_t)
    t = [eye_w + side_by_side(a) for a in a_ab]
    pd = [a.astype(BF16) for a in a_ab]
    pw = [side_by_side(a).astype(BF16) for a in a_ab]
    akv = [_dot(a_ak[u], v16[units[u][0]]).astype(BF16) for u in n_units]
    rkv = [_dot(a_rk[u], v16[units[u][0]]) for u in n_units]
    for _ in range(L.bit_length() - 2):
        pw = [_dot(pw[u], pd[u]).astype(BF16) for u in n_units]
        pd = [block_diag(pw[u]) for u in n_units]
        t = [t[u] + _dot(t[u].astype(BF16), pd[u]) for u in n_units]
    tau = [_dot(block_diag(t[u].astype(BF16)),
                jnp.concatenate([at16[units[u][0]], akv[u]], axis=1)).astype(BF16) for u in n_units]
    rpy = [_dot(a_rb[u], tau[u]) for u in n_units]

    ta16, u16, rp16, yi = [], [], [], []
    for p in prs:
        u0 = p * HEADS_PER_BLOCK
        tau_p = sel(tau[u0], tau[u0 + 1])
        rpy_p = sel(rpy[u0], rpy[u0 + 1])
        ta16.append(tau_p[:, :LANES])
        u16.append(tau_p[:, LANES:])
        rp16.append((rt[p] + rpy_p[:, :LANES]).astype(BF16))
        yi.append(rpy_p[:, LANES:] + sel(rkv[u0], rkv[u0 + 1]))

    mts, gts = [], []
    for c in range(chunks):
        rows = slice(c * L, (c + 1) * L)
        mts.append([(jnp.where(eye, jnp.exp(cum_end[p][c]), 0.0)
                     + bd * _dot_tn(bh16[p][rows], ta16[p][rows])).astype(BF16) for p in prs])
        gts.append([bd * (_dot_tn(bh16[p][rows], u16[p][rows]) + _dot_tn(kh16[p][rows], v16[p][rows]))
                    for p in prs])
    st = [st_ref[p] for p in prs]
    ys = [[] for _ in prs]
    for c in range(chunks):
        rows = slice(c * L, (c + 1) * L)
        st16 = [s.astype(BF16) for s in st]
        for p in prs:
            ys[p].append(_dot(rp16[p][rows], st16[p]) + yi[p][rows])
        st = [_dot(mts[c][p], st16[p]) + gts[c][p] for p in prs]
    for p in prs:
        st_ref[p] = st[p]
        y = jnp.concatenate(ys[p], axis=0)
        mu = _dot_mask_rhs(y, bd, 1) * (1.0 / HEAD_DIM)
        d = y - mu
        var = _dot_mask_rhs(d * d, bd, 1) * (1.0 / HEAD_DIM)
        yn = d * lax.rsqrt(var + GN_EPS) * gw_ref[:, lanes[p]] + gb_ref[:, lanes[p]]
        bonus = _dot_mask_rhs(r[p] * k[p] * rk_ref[:, lanes[p]], bd, 1) * v[p]
        y_ref[0, :, lanes[p]] = yn + bonus


def _wkv(r, lw, k, v, kk, a, r_k, lnx_w, lnx_b):
    bsz, seq, c = r.shape
    chunks, pairs = WKV_CHUNKS_PER_STEP, WKV_PAIRS_PER_STEP
    tl = chunks * WKV_CHUNK
    width = pairs * LANES
    assert seq % tl == 0 and c % width == 0
    tok = pl.BlockSpec((1, tl, width), lambda b, h, t: (b, t, h))
    par = pl.BlockSpec((1, width), lambda b, h, t: (0, h))
    return pl.pallas_call(
        functools.partial(_wkv_kernel, chunks=chunks, pairs=pairs),
        out_shape=jax.ShapeDtypeStruct((bsz, seq, c), F32),
        grid=(bsz, c // width, seq // tl),
        in_specs=[tok] * 6 + [par] * 3,
        out_specs=tok,
        scratch_shapes=[pltpu.VMEM((pairs, LANES, LANES), F32)],
        compiler_params=_params("parallel", "parallel", "arbitrary"),
        name="wkv",
    )(r, lw, k, v, kk, a, r_k, lnx_w, lnx_b)


def _proj_res_kernel(*refs, gated):
    if gated:
        x_ref, g_ref, w_ref, res_ref, o_ref = refs
        x = (x_ref[...] * g_ref[...]).astype(BF16)
    else:
        x_ref, w_ref, res_ref, o_ref = refs
        x = x_ref[...]
    o_ref[...] = res_ref[...] + _dot(x, w_ref[...])


def _proj_res(x, gate, w, res):
    m, c = res.shape
    tm = TM_PROJ
    row = pl.BlockSpec((tm, c), lambda i: (i, 0))
    gated = gate is not None
    ins = [x, gate, w, res] if gated else [x, w, res]
    specs = [row, row, _full(w.shape), row] if gated else [row, _full(w.shape), row]
    return pl.pallas_call(
        functools.partial(_proj_res_kernel, gated=gated),
        out_shape=jax.ShapeDtypeStruct((m, c), F32),
        grid=(m // tm,),
        in_specs=specs,
        out_specs=row,
        compiler_params=_params("parallel"),
        name="proj_res",
    )(*ins)


def _moe_kernel(h_ref, gn_ref, wr_ref, br_ref, wg_ref, wu_ref, wd_ref, gf_ref, o_ref,
                xn_ref, gates_ref, acc_ref, *, final_norm, d_expert):
    g = pl.program_id(1)
    lane = lax.broadcasted_iota(jnp.int32, (1, LANES), 1)

    @pl.when(g == 0)
    def _():
        xn = _rms(h_ref[...]) * gn_ref[...]
        xn_ref[...] = xn.astype(BF16)
        logits = _dot_3pass(xn, wr_ref[...]) + br_ref[...]
        neg = jnp.float32(-jnp.inf)
        is_g = lane < N_GROUPS
        gl = jnp.where(is_g, logits, neg)
        gmax = jnp.max(gl, axis=-1, keepdims=True)
        gval = 1.0 / jnp.sum(jnp.exp(gl - gmax), axis=-1, keepdims=True)
        gidx = jnp.min(jnp.where(gl == gmax, lane, LANES), axis=-1, keepdims=True)
        ex = lane - N_GROUPS
        in_grp = (ex >= gidx * EXPERTS_PER_GROUP) & (ex < (gidx + 1) * EXPERTS_PER_GROUP)
        il = jnp.where(in_grp, logits, neg)
        l1 = jnp.max(il, axis=-1, keepdims=True)
        i1 = jnp.min(jnp.where(il == l1, lane, LANES), axis=-1, keepdims=True)
        il2 = jnp.where(lane == i1, neg, il)
        l2 = jnp.max(il2, axis=-1, keepdims=True)
        i2 = jnp.min(jnp.where(il2 == l2, lane, LANES), axis=-1, keepdims=True)
        e2 = jnp.exp(l2 - l1)
        w1 = gval / (1.0 + e2)
        w2 = gval * e2 / (1.0 + e2)
        gates_ref[...] = jnp.where(lane == i1, w1, 0.0) + jnp.where(lane == i2, w2, 0.0)
        acc_ref[...] = jnp.zeros_like(acc_ref)

    xn = xn_ref[...]
    gates = gates_ref[...]
    first = N_GROUPS + g * EXPERTS_PER_GROUP
    hm = []
    for e in range(EXPERTS_PER_GROUP):
        hg = _dot(xn, wg_ref[e])
        hu = _dot(xn, wu_ref[e])
        ge = jnp.sum(jnp.where(lane == first + e, gates, 0.0), axis=-1, keepdims=True)
        hm.append(((hg * _sigmoid(hg)) * hu * ge).astype(BF16))
    wd = wd_ref[...].reshape(EXPERTS_PER_GROUP * d_expert, wd_ref.shape[-1])
    acc_ref[...] += _dot(jnp.concatenate(hm, axis=1), wd)

    @pl.when(g == N_GROUPS - 1)
    def _():
        out = h_ref[...] + acc_ref[...]
        if final_norm:
            out = _rms(out) * gf_ref[...]
        o_ref[...] = out


def _moe(h, gn, w_router, b_router, w_gate, w_up, w_down, g_final, final_norm):
    m, c = h.shape
    tm = TM_MOE
    f = w_gate.shape[-1]
    per = EXPERTS_PER_GROUP
    row = pl.BlockSpec((tm, c), lambda i, g: (i, 0))
    return pl.pallas_call(
        functools.partial(_moe_kernel, final_norm=final_norm, d_expert=f),
        out_shape=jax.ShapeDtypeStruct((m, c), F32),
        grid=(m // tm, N_GROUPS),
        in_specs=[row, _full(gn.shape), _full(w_router.shape), _full(b_router.shape),
                  pl.BlockSpec((per, c, f), lambda i, g: (g, 0, 0)),
                  pl.BlockSpec((per, c, f), lambda i, g: (g, 0, 0)),
                  pl.BlockSpec((per, f, c), lambda i, g: (g, 0, 0)),
                  _full(g_final.shape)],
        out_specs=row,
        scratch_shapes=[pltpu.VMEM((tm, c), BF16), pltpu.VMEM((tm, LANES), F32),
                        pltpu.VMEM((tm, c), F32)],
        compiler_params=_params("parallel", "arbitrary"),
        name="moe",
    )(h, gn, w_router, b_router, w_gate, w_up, w_down, g_final)


def _norm_proj_kernel(h_ref, gn_ref, w_ref, o_ref, *, scale):
    xn = (_rms(h_ref[...]) * gn_ref[...]).astype(BF16)
    o_ref[...] = (_dot(xn, w_ref[...]) * scale).astype(o_ref.dtype)


def _norm_proj(h, gn, w, scale):
    m, c = h.shape
    tm = TM_PROJ
    n = w.shape[1]
    return pl.pallas_call(
        functools.partial(_norm_proj_kernel, scale=scale),
        out_shape=jax.ShapeDtypeStruct((m, n), BF16),
        grid=(m // tm,),
        in_specs=[pl.BlockSpec((tm, c), lambda i: (i, 0)), _full(gn.shape), _full(w.shape)],
        out_specs=pl.BlockSpec((tm, n), lambda i: (i, 0)),
        compiler_params=_params("parallel"),
        name="norm_proj",
    )(h, gn, w)


MASK_BIAS = -1e30
SOFTPLUS_LINEAR = 100.0
ATTN_SETTLED = 160.0


def _sb_attn_kernel(q_ref, k_ref, v_ref, o_ref, carry_ref, acc_ref, bias_ref, qh_ref, vh_ref,
                    *slot_refs, tq, nq):
    z_ref, e_ref, tot_ref, att_ref = zip(slot_refs[:4], slot_refs[4:])
    tk = tq
    heads = range(HEADS_PER_BLOCK)
    lane = lax.broadcasted_iota(jnp.int32, (1, LANES), 1)
    head0 = lane < HEAD_DIM
    jr = lax.broadcasted_iota(jnp.int32, (tk, tk), 0)
    sc = lax.broadcasted_iota(jnp.int32, (tk, tk), 1)
    suffix_mat = (jr > sc).astype(BF16)
    ones_mat = jnp.ones((tk, LANES), BF16)
    zero = jnp.zeros((), BF16)

    def rows(ref, idx, n):
        return ref[pl.ds(pl.multiple_of(idx * n, n), n), :]

    for ref in slot_refs + (carry_ref, acc_ref):
        ref[...] = jnp.zeros_like(ref)
    bias_ref[0] = jnp.zeros((tq, tk), F32)
    bias_ref[1] = jnp.where(sc < jr, 0.0, MASK_BIAS)
    q = q_ref[0]
    v = v_ref[0]
    seq = nq * tq
    qh_ref[0] = jnp.where(head0, q, zero)
    qh_ref[1] = jnp.where(head0, zero, q)
    vh_ref[0, :seq] = jnp.where(head0, v, zero)
    vh_ref[1, :seq] = jnp.where(head0, zero, v)
    vh_ref[:, seq:] = jnp.zeros((HEADS_PER_BLOCK, tk, LANES), BF16)

    def stage0(t, kv, slot):
        kb = rows(k_ref.at[0], jnp.minimum(kv, nq - 1), tk)
        bias = bias_ref[(kv == t).astype(jnp.int32)]
        for h in heads:
            z_ref[slot][h] = _dot_nt(rows(qh_ref.at[h], t, tq), kb) + bias

    def stage1_head(slot):
        sp16, zs = [], []
        for h in heads:
            z = z_ref[slot][h]
            sp = jnp.maximum(z, jnp.log(1.0 + jnp.exp2(jnp.minimum(z, SOFTPLUS_LINEAR))) * LOG2E)
            sp16.append(sp.astype(BF16))
            zs.append(z - sp)
        return sp16, zs

    def stage1_tail(slot, sp16, zs):
        for h in heads:
            e_ref[slot][h] = zs[h] - _dot(sp16[h], suffix_mat)
            tot_ref[slot][h] = _dot(sp16[h], ones_mat)

    def stage2(t, kv, slot):
        first = kv == t
        low = None
        for h in heads:
            carry = jnp.where(first, 0.0, carry_ref[slot, h])
            new = carry + tot_ref[slot][h]
            carry_ref[slot, h] = new
            low = new if low is None else jnp.minimum(low, new)
            att = jnp.exp2(e_ref[slot][h] - jnp.concatenate([carry] * (tk // LANES), axis=1))
            att_ref[slot][h] = att.astype(BF16)
        return (jnp.min(low) >= ATTN_SETTLED) & (kv != nq)

    def stage3(t, kv, slot):
        pv = (_dot(att_ref[slot][0], rows(vh_ref.at[0], kv, tk))
              + _dot(att_ref[slot][1], rows(vh_ref.at[1], kv, tk)))
        acc = jnp.where(kv == t, 0.0, acc_ref[slot]) + pv
        acc_ref[slot] = acc
        o_ref[0, pl.ds(pl.multiple_of(t * tq, tq), tq), :] = acc.astype(o_ref.dtype)

    def advance(t0, k0, t2, settled, t_end):
        tile_done = (k0 == 0) | (settled & (t0 == t2))
        nt = jnp.where(tile_done, t0 + 1, t0)
        ended = (k0 == nq) | (nt == t_end)
        return (jnp.where(ended, t_end - 1, nt),
                jnp.where(ended, nq, jnp.where(tile_done, nt, k0 - 1)))

    half = nq // 2
    t_end = (half, nq)

    def step(cur, state):
        nxt, (b1, b2, b3), n_ended = state
        old = 1 - cur
        t0, k0 = nxt[cur]
        stage0(t0, k0, cur)
        stage3(*b3, old)
        held = stage1_head(old)
        settled = stage2(*b2, cur)
        stage1_tail(old, *held)
        moved = advance(t0, k0, b2[0], settled, t_end[cur])
        nxt = (moved, nxt[1]) if cur == 0 else (nxt[0], moved)
        ended = (k0 == nq).astype(jnp.int32)
        n_ended = (n_ended[0] + ended, n_ended[1]) if cur == 0 else (n_ended[0], n_ended[1] + ended)
        return nxt, ((t0, k0), b1, b2), n_ended

    def trip(state):
        return step(1, step(0, state))

    i32 = lambda x: jnp.int32(x)
    idle = ((i32(half - 1), i32(nq)), (i32(nq - 1), i32(nq)))
    start = ((i32(0), i32(0)), (i32(half), i32(half)))
    lax.while_loop(lambda s: (s[2][0] < 2) | (s[2][1] < 2), trip,
                   (start, (idle[1], idle[0], idle[1]), (i32(0), i32(0))))


def _sb_attn(q, kv):
    bsz, seq, c = q.shape
    tq = ATTN_TILE
    assert seq % (2 * tq) == 0
    nhb = c // LANES
    qspec = pl.BlockSpec((1, seq, LANES), lambda b, h: (b, 0, h))
    vspec = pl.BlockSpec((1, seq, LANES), lambda b, h: (b, 0, nhb + h))
    blk = (HEADS_PER_BLOCK, tq, tq)
    slot = [pltpu.VMEM(blk, F32), pltpu.VMEM(blk, F32),
            pltpu.VMEM((HEADS_PER_BLOCK, tq, LANES), F32), pltpu.VMEM(blk, BF16)]
    return pl.pallas_call(
        functools.partial(_sb_attn_kernel, tq=tq, nq=seq // tq),
        out_shape=jax.ShapeDtypeStruct((bsz, seq, c), BF16),
        grid=(bsz, nhb),
        in_specs=[qspec, qspec, vspec],
        out_specs=qspec,
        scratch_shapes=[pltpu.VMEM((2, HEADS_PER_BLOCK, tq, LANES), F32), pltpu.VMEM((2, tq, LANES), F32),
                        pltpu.VMEM((2, tq, tq), F32),
                        pltpu.VMEM((HEADS_PER_BLOCK, seq, LANES), BF16),
                        pltpu.VMEM((HEADS_PER_BLOCK, seq + tq, LANES), BF16)] + slot + slot,
        compiler_params=_params("parallel", "parallel"),
        name="sb_attn",
    )(q, kv, kv)


def _router_params(w_group, b_group, w_inner, b_inner):
    c = w_group.shape[0]
    w = jnp.concatenate([w_group, w_inner.reshape(c, N_EXPERTS)], axis=1)
    b = jnp.concatenate([b_group, b_inner.reshape(N_EXPERTS)])
    pad = LANES - w.shape[1]
    return jnp.pad(w, ((0, 0), (0, pad))), jnp.pad(b, (0, pad)).reshape(1, LANES)


def kernel(x, norm_mix, norm_ffn, norm_kv, norm_final, rw_mix, rw_w_rkv, rw_w0, rw_w1, rw_w2, rw_a0, rw_a1, rw_a2, rw_g1, rw_g2, rw_k_k, rw_k_a, rw_r_k, rw_lnx_w, rw_lnx_b, rw_w_out, w_kv, sb_w_q, sb_w_out, moe_w_group, moe_b_group, moe_w_inner, moe_b_inner, moe_w_gate, moe_w_up, moe_w_down):
    bsz, seq, c = x.shape
    depth = norm_mix.shape[0]
    n_a = rw_mix.shape[0]
    m = bsz * seq
    vec = lambda t: t.reshape(1, c)
    bf = lambda t: t.astype(BF16)

    h = x.reshape(m, c)
    kv_sh = None
    for l in range(depth):
        if l < n_a:
            i = l
            r, k, v, lw, kk, a, g = _rwkv_pre(
                h, seq, vec(norm_mix[l]), jnp.pad(rw_mix[i], ((0, 2), (0, 0))), bf(rw_w_rkv[i]),
                vec(rw_w0[i]), bf(rw_w1[i]), bf(rw_w2[i]), vec(rw_a0[i]), bf(rw_a1[i]), bf(rw_a2[i]),
                bf(rw_g1[i]), bf(rw_g2[i]), vec(rw_k_k[i]), vec(rw_k_a[i]))
            s3 = lambda t: t.reshape(bsz, seq, c)
            y = _wkv(s3(r), s3(lw), s3(k), s3(v), s3(kk), s3(a), vec(rw_r_k[i]),
                     vec(rw_lnx_w[i]), vec(rw_lnx_b[i]))
            h = _proj_res(y.reshape(m, c), g, bf(rw_w_out[i]), h)
        else:
            j = l - n_a
            q = _norm_proj(h, vec(norm_mix[l]), bf(sb_w_q[j]), LOG2E * HEAD_DIM ** -0.5)
            o = _sb_attn(q.reshape(bsz, seq, c), kv_sh)
            h = _proj_res(o.reshape(m, c), None, bf(sb_w_out[j]), h)
        w_r, b_r = _router_params(moe_w_group[l], moe_b_group[l], moe_w_inner[l], moe_b_inner[l])
        h = _moe(h, vec(norm_ffn[l]), w_r, b_r, bf(moe_w_gate[l]), bf(moe_w_up[l]), bf(moe_w_down[l]),
                 vec(norm_final), l == depth - 1)
        if l == n_a - 1:
            kv_sh = _norm_proj(h, vec(norm_kv), bf(w_kv), 1.0).reshape(bsz, seq, 2 * c)
    return h.reshape(bsz, seq, c)
```

```python
import functools

import jax
import jax.numpy as jnp
from jax import lax
from jax.experimental import pallas as pl
from jax.experimental.pallas import tpu as pltpu

F32 = jnp.float32
BF16 = jnp.bfloat16

LANES = 128
HEAD_DIM = 64
HEADS_PER_BLOCK = LANES // HEAD_DIM
RMS_EPS = 1e-6
GN_EPS = 64e-5
N_GROUPS = 4
EXPERTS_PER_GROUP = 4
N_EXPERTS = N_GROUPS * EXPERTS_PER_GROUP
WKV_CHUNK = 64
VMEM_LIMIT = 56 * 1024 * 1024
LOG2E = 1.4426950408889634

TM_RWKV_PRE = 512
TM_PROJ = 512
TM_MOE = 1024
WKV_CHUNKS_PER_STEP = 4
WKV_PAIRS_PER_STEP = 4
ATTN_TILE = 2 * LANES


def _params(*sem):
    return pltpu.CompilerParams(dimension_semantics=sem, vmem_limit_bytes=VMEM_LIMIT)


def _dot(a, b):
    return lax.dot_general(a, b, (((1,), (0,)), ((), ())), preferred_element_type=F32)


def _dot_nt(a, b):
    return lax.dot_general(a, b, (((1,), (1,)), ((), ())), preferred_element_type=F32)


def _dot_tn(a, b):
    return lax.dot_general(a, b, (((0,), (0,)), ((), ())), preferred_element_type=F32)


def _bf16_terms(x, n):
    terms = []
    for _ in range(n):
        t = x.astype(BF16)
        terms.append(t)
        x = x - t.astype(F32)
    return terms


def _dot_mask_rhs(x, mask01, n):
    m = mask01.astype(BF16)
    return sum(_dot(t, m) for t in _bf16_terms(x, n))


def _dot_mask_lhs(mask01, x, n):
    m = mask01.astype(BF16)
    return sum(_dot(m, t) for t in _bf16_terms(x, n))


def _dot_3pass_nt(a, b):
    a_hi, a_lo = _bf16_terms(a, 2)
    b_hi, b_lo = _bf16_terms(b, 2)
    return _dot_nt(a_hi, b_hi) + _dot_nt(a_hi, b_lo) + _dot_nt(a_lo, b_hi)


def _rms(x):
    return x * lax.rsqrt(jnp.mean(x * x, axis=-1, keepdims=True) + RMS_EPS)


def _sigmoid(x):
    return 1.0 / (1.0 + jnp.exp(-x))


def _softplus(x):
    return jnp.maximum(x, 0.0) + jnp.log(1.0 + jnp.exp(-jnp.abs(x)))


def _full(shape):
    nd = len(shape)
    return pl.BlockSpec(shape, lambda *_: (0,) * nd)


def _rwkv_pre_kernel(h_ref, hp_ref, gn_ref, mix_ref, wrkv_ref, w0_ref, w1_ref, w2_ref,
                     a0_ref, a1_ref, a2_ref, g1_ref, g2_ref, kk_ref, ka_ref,
                     r_out, k_out, v_out, lw_out, kk_out, a_out, g_out, *, tiles_per_seq):
    i = pl.program_id(0)
    gn = gn_ref[...]
    hn = _rms(h_ref[...]) * gn
    hpn = _rms(hp_ref[7:8, :]) * gn
    hpn = jnp.where(i % tiles_per_seq == 0, 0.0, hpn)
    row = lax.broadcasted_iota(jnp.int32, hn.shape, 0)
    prev = jnp.where(row == 0, hpn, pltpu.roll(hn, 1, axis=0))
    xx = prev - hn

    def mixed(n):
        return (hn + xx * mix_ref[n:n + 1, :]).astype(BF16)

    r = _dot(mixed(0), wrkv_ref[0])
    k = _dot(mixed(1), wrkv_ref[1])
    v = _dot(mixed(2), wrkv_ref[2])
    wl = w0_ref[...] + _dot(jnp.tanh(_dot(mixed(3), w1_ref[...])).astype(BF16), w2_ref[...])
    a = _sigmoid(a0_ref[...] + _dot(_dot(mixed(4), a1_ref[...]).astype(BF16), a2_ref[...]))
    g = _dot(_sigmoid(_dot(mixed(5), g1_ref[...])).astype(BF16), g2_ref[...])
    r_out[...] = r
    k_out[...] = k * (1.0 + (a - 1.0) * ka_ref[...])
    v_out[...] = v
    lw_out[...] = -jnp.exp(-_softplus(-wl) - 0.5)
    kk_out[...] = k * kk_ref[...]
    a_out[...] = a
    g_out[...] = g


def _rwkv_pre(h, seq, gn, mix, wrkv, w0, w1, w2, a0, a1, a2, g1, g2, k_k, k_a):
    m, c = h.shape
    tm = TM_RWKV_PRE
    assert seq % tm == 0 and m % seq == 0
    row = lambda i: (i, 0)
    out = jax.ShapeDtypeStruct((m, c), F32)
    ins = [gn, mix, wrkv, w0, w1, w2, a0, a1, a2, g1, g2, k_k, k_a]
    return pl.pallas_call(
        functools.partial(_rwkv_pre_kernel, tiles_per_seq=seq // tm),
        out_shape=[out] * 7,
        grid=(m // tm,),
        in_specs=[pl.BlockSpec((tm, c), row),
                  pl.BlockSpec((8, c), lambda i: (jnp.maximum(i * (tm // 8) - 1, 0), 0))]
                 + [_full(t.shape) for t in ins],
        out_specs=[pl.BlockSpec((tm, c), row)] * 7,
        compiler_params=_params("parallel"),
        name="rwkv_pre",
    )(h, h, *ins)


def _wkv_kernel(r_ref, lw_ref, k_ref, v_ref, kk_ref, a_ref, rk_ref, gw_ref, gb_ref,
                y_ref, st_ref, *, chunks, pairs):
    L = WKV_CHUNK

    @pl.when(pl.program_id(2) == 0)
    def _():
        st_ref[...] = jnp.zeros_like(st_ref)

    tl = chunks * L
    lane = lax.broadcasted_iota(jnp.int32, (1, LANES), 1)
    head0 = lane < HEAD_DIM
    rr = lax.broadcasted_iota(jnp.int32, (LANES, LANES), 0)
    cc = lax.broadcasted_iota(jnp.int32, (LANES, LANES), 1)
    bd = ((rr < HEAD_DIM) == (cc < HEAD_DIM)).astype(F32)
    eye = rr == cc
    tr = lax.broadcasted_iota(jnp.int32, (tl, tl), 0)
    tc = lax.broadcasted_iota(jnp.int32, (tl, tl), 1)
    shift = L.bit_length() - 1
    same_chunk = (tr >> shift) == (tc >> shift)
    strict = same_chunk & (tr > tc)
    incl = same_chunk & (tr >= tc)
    tri = incl.astype(F32)
    eye_t = (tr == tc).astype(F32)
    m0 = head0.astype(F32)
    masks = (m0, 1.0 - m0)
    head0w = jnp.concatenate([head0, head0], axis=1)
    sel = lambda x0, x1: jnp.where(head0 if x0.shape[1] == LANES else head0w, x0, x1)

    prs = range(pairs)
    heads = range(HEADS_PER_BLOCK)
    units = [(p, h) for p in prs for h in heads]
    lanes = [slice(p * LANES, (p + 1) * LANES) for p in prs]

    r, v, k, at, at16, bt16, kt16, rt, bh16, kh16, v16, cum_end = ([] for _ in range(12))
    for p in prs:
        r_p = r_ref[0, :, lanes[p]]
        lw = lw_ref[0, :, lanes[p]]
        k_p = k_ref[0, :, lanes[p]]
        v_p = v_ref[0, :, lanes[p]]
        kkr = kk_ref[0, :, lanes[p]]
        ag = a_ref[0, :, lanes[p]]
        cum = _dot_mask_lhs(tri, lw, 3)
        ends = [cum[(c + 1) * L - 1:(c + 1) * L, :] for c in range(chunks)]
        cum_l = jnp.concatenate([jnp.broadcast_to(e, (L, LANES)) for e in ends], axis=0)
        w_inv = jnp.exp(-cum)
        w_tail = jnp.exp(cum_l - cum)
        kk = kkr / jnp.maximum(jnp.sqrt(_dot_mask_rhs(kkr * kkr, bd, 1)), 1e-12)
        b = kk * ag
        at_p = -kk * jnp.exp(cum - lw)
        r.append(r_p)
        k.append(k_p)
        v.append(v_p)
        cum_end.append(ends)
        at.append(at_p)
        at16.append(at_p.astype(BF16))
        bt16.append((b * w_inv).astype(BF16))
        kt16.append((k_p * w_inv).astype(BF16))
        rt.append(r_p * jnp.exp(cum))
        bh16.append((b * w_tail).astype(BF16))
        kh16.append((k_p * w_tail).astype(BF16))
        v16.append(v_p.astype(BF16))

    a_ab, a_ak, a_rb, a_rk = [], [], [], []
    for p, h in units:
        lhs = jnp.concatenate([at[p] * masks[h], rt[p] * masks[h]], axis=0).astype(BF16)
        pb = _dot_nt(lhs, bt16[p])
        pk = _dot_nt(lhs, kt16[p])
        a_ab.append(jnp.where(strict, pb[:tl], 0.0))
        a_ak.append(jnp.where(strict, pk[:tl], 0.0).astype(BF16))
        a_rb.append(jnp.where(incl, pb[tl:], 0.0).astype(BF16))
        a_rk.append(jnp.where(incl, pk[tl:], 0.0).astype(BF16))
    def side_by_side(x):
        return sum(x[c * L:(c + 1) * L] for c in range(chunks))

    def block_diag(x16):
        return jnp.where(same_chunk, jnp.concatenate([x16] * chunks, axis=0), jnp.zeros((), BF16))

    n_units = range(len(units))
    eye_w = side_by_side(eye_t)
    t = [eye_w + side_by_side(a) for a in a_ab]
    pd = [a.astype(BF16) for a in a_ab]
    pw = [side_by_side(a).astype(BF16) for a in a_ab]
    akv = [_dot(a_ak[u], v16[units[u][0]]).astype(BF16) for u in n_units]
    rkv = [_dot(a_rk[u], v16[units[u][0]]) for u in n_units]
    for _ in range(L.bit_length() - 2):
        pw = [_dot(pw[u], pd[u]).astype(BF16) for u in n_units]
        pd = [block_diag(pw[u]) for u in n_units]
        t = [t[u] + _dot(t[u].astype(BF16), pd[u]) for u in n_units]
    tau = [_dot(block_diag(t[u].astype(BF16)),
                jnp.concatenate([at16[units[u][0]], akv[u]], axis=1)).astype(BF16) for u in n_units]
    rpy = [_dot(a_rb[u], tau[u]) for u in n_units]

    ta16, u16, rp16, yi = [], [], [], []
    for p in prs:
        u0 = p * HEADS_PER_BLOCK
        tau_p = sel(tau[u0], tau[u0 + 1])
        rpy_p = sel(rpy[u0], rpy[u0 + 1])
        ta16.append(tau_p[:, :LANES])
        u16.append(tau_p[:, LANES:])
        rp16.append((rt[p] + rpy_p[:, :LANES]).astype(BF16))
        yi.append(rpy_p[:, LANES:] + sel(rkv[u0], rkv[u0 + 1]))

    mts, gts = [], []
    for c in range(chunks):
        rows = slice(c * L, (c + 1) * L)
        mts.append([(jnp.where(eye, jnp.exp(cum_end[p][c]), 0.0)
                     + bd * _dot_tn(bh16[p][rows], ta16[p][rows])).astype(BF16) for p in prs])
        gts.append([bd * (_dot_tn(bh16[p][rows], u16[p][rows]) + _dot_tn(kh16[p][rows], v16[p][rows]))
                    for p in prs])
    st = [st_ref[p] for p in prs]
    ys = [[] for _ in prs]
    for c in range(chunks):
        rows = slice(c * L, (c + 1) * L)
        st16 = [s.astype(BF16) for s in st]
        for p in prs:
            ys[p].append(_dot(rp16[p][rows], st16[p]) + yi[p][rows])
        st = [_dot(mts[c][p], st16[p]) + gts[c][p] for p in prs]
    for p in prs:
        st_ref[p] = st[p]
        y = jnp.concatenate(ys[p], axis=0)
        mu = _dot_mask_rhs(y, bd, 1) * (1.0 / HEAD_DIM)
        d = y - mu
        var = _dot_mask_rhs(d * d, bd, 1) * (1.0 / HEAD_DIM)
        yn = d * lax.rsqrt(var + GN_EPS) * gw_ref[:, lanes[p]] + gb_ref[:, lanes[p]]
        bonus = _dot_mask_rhs(r[p] * k[p] * rk_ref[:, lanes[p]], bd, 1) * v[p]
        y_ref[0, :, lanes[p]] = yn + bonus


def _wkv(r, lw, k, v, kk, a, r_k, lnx_w, lnx_b):
    bsz, seq, c = r.shape
    chunks, pairs = WKV_CHUNKS_PER_STEP, WKV_PAIRS_PER_STEP
    tl = chunks * WKV_CHUNK
    width = pairs * LANES
    assert seq % tl == 0 and c % width == 0
    tok = pl.BlockSpec((1, tl, width), lambda b, h, t: (b, t, h))
    par = pl.BlockSpec((1, width), lambda b, h, t: (0, h))
    return pl.pallas_call(
        functools.partial(_wkv_kernel, chunks=chunks, pairs=pairs),
        out_shape=jax.ShapeDtypeStruct((bsz, seq, c), F32),
        grid=(bsz, c // width, seq // tl),
        in_specs=[tok] * 6 + [par] * 3,
        out_specs=tok,
        scratch_shapes=[pltpu.VMEM((pairs, LANES, LANES), F32)],
        compiler_params=_params("parallel", "parallel", "arbitrary"),
        name="wkv",
    )(r, lw, k, v, kk, a, r_k, lnx_w, lnx_b)


def _proj_res_kernel(*refs, gated):
    if gated:
        x_ref, g_ref, w_ref, res_ref, o_ref = refs
        x = (x_ref[...] * g_ref[...]).astype(BF16)
    else:
        x_ref, w_ref, res_ref, o_ref = refs
        x = x_ref[...]
    o_ref[...] = res_ref[...] + _dot(x, w_ref[...])


def _proj_res(x, gate, w, res):
    m, c = res.shape
    tm = TM_PROJ
    row = pl.BlockSpec((tm, c), lambda i: (i, 0))
    gated = gate is not None
    ins = [x, gate, w, res] if gated else [x, w, res]
    specs = [row, row, _full(w.shape), row] if gated else [row, _full(w.shape), row]
    return pl.pallas_call(
        functools.partial(_proj_res_kernel, gated=gated),
        out_shape=jax.ShapeDtypeStruct((m, c), F32),
        grid=(m // tm,),
        in_specs=specs,
        out_specs=row,
        compiler_params=_params("parallel"),
        name="proj_res",
    )(*ins)


def _moe_kernel(h_ref, gn_ref, wr_ref, br_ref, wg_ref, wu_ref, wd_ref, gf_ref, o_ref,
                xn_ref, gates_ref, acc_ref, *, final_norm, d_expert):
    g = pl.program_id(1)
    lane = lax.broadcasted_iota(jnp.int32, (1, LANES), 1)

    @pl.when(g == 0)
    def _():
        xn = _rms(h_ref[...]) * gn_ref[...]
        xn_ref[...] = xn.astype(BF16)
        lt = _dot_3pass_nt(wr_ref[...], xn) + br_ref[...]
        tm = lt.shape[1]
        per = EXPERTS_PER_GROUP

        def max4(x):
            return jnp.maximum(jnp.maximum(x[0], x[1]), jnp.maximum(x[2], x[3]))

        def first_at(x, m):
            return jnp.where(x[0] == m, 0, jnp.where(x[1] == m, 1, jnp.where(x[2] == m, 2, 3)))

        gl = [lt[r:r + 1, :] for r in range(N_GROUPS)]
        gmax = max4(gl)
        gval = 1.0 / sum(jnp.exp(x - gmax) for x in gl)
        gidx = first_at(gl, gmax)
        inner = [[lt[N_GROUPS + gi * per + e:N_GROUPS + gi * per + e + 1, :] for gi in range(N_GROUPS)]
                 for e in range(per)]
        s = [jnp.where(gidx == 0, x[0], jnp.where(gidx == 1, x[1], jnp.where(gidx == 2, x[2], x[3])))
             for x in inner]
        l1 = max4(s)
        i1 = first_at(s, l1)
        s2 = [jnp.where(i1 == e, -jnp.inf, s[e]) for e in range(per)]
        l2 = max4(s2)
        i2 = first_at(s2, l2)
        e2 = jnp.exp(l2 - l1)
        w1 = gval / (1.0 + e2)
        w2 = gval * e2 / (1.0 + e2)
        rid = lax.broadcasted_iota(jnp.int32, (N_EXPERTS, tm), 0)
        gt = (jnp.where(rid == gidx * per + i1, w1, 0.0) + jnp.where(rid == gidx * per + i2, w2, 0.0))
        gt = jnp.concatenate([gt, jnp.zeros((LANES - N_EXPERTS, tm), F32)], axis=0)
        gates_ref[...] = gt.T
        acc_ref[...] = jnp.zeros_like(acc_ref)

    xn = xn_ref[...]
    gates = gates_ref[...]
    first = g * EXPERTS_PER_GROUP
    hm = []
    for e in range(EXPERTS_PER_GROUP):
        hg = _dot(xn, wg_ref[e])
        hu = _dot(xn, wu_ref[e])
        ge = jnp.sum(jnp.where(lane == first + e, gates, 0.0), axis=-1, keepdims=True)
        hm.append(((hg * _sigmoid(hg)) * hu * ge).astype(BF16))
    wd = wd_ref[...].reshape(EXPERTS_PER_GROUP * d_expert, wd_ref.shape[-1])
    acc_ref[...] += _dot(jnp.concatenate(hm, axis=1), wd)

    @pl.when(g == N_GROUPS - 1)
    def _():
        out = h_ref[...] + acc_ref[...]
        if final_norm:
            out = _rms(out) * gf_ref[...]
        o_ref[...] = out


def _moe(h, gn, w_router, b_router, w_gate, w_up, w_down, g_final, final_norm):
    m, c = h.shape
    tm = TM_MOE
    f = w_gate.shape[-1]
    per = EXPERTS_PER_GROUP
    row = pl.BlockSpec((tm, c), lambda i, g: (i, 0))
    return pl.pallas_call(
        functools.partial(_moe_kernel, final_norm=final_norm, d_expert=f),
        out_shape=jax.ShapeDtypeStruct((m, c), F32),
        grid=(m // tm, N_GROUPS),
        in_specs=[row, _full(gn.shape), _full(w_router.shape), _full(b_router.shape),
                  pl.BlockSpec((per, c, f), lambda i, g: (g, 0, 0)),
                  pl.BlockSpec((per, c, f), lambda i, g: (g, 0, 0)),
                  pl.BlockSpec((per, f, c), lambda i, g: (g, 0, 0)),
                  _full(g_final.shape)],
        out_specs=row,
        scratch_shapes=[pltpu.VMEM((tm, c), BF16), pltpu.VMEM((tm, LANES), F32),
                        pltpu.VMEM((tm, c), F32)],
        compiler_params=_params("parallel", "arbitrary"),
        name="moe",
    )(h, gn, w_router, b_router, w_gate, w_up, w_down, g_final)


def _norm_proj_kernel(h_ref, gn_ref, w_ref, o_ref, *, scale):
    xn = (_rms(h_ref[...]) * gn_ref[...]).astype(BF16)
    o_ref[...] = (_dot(xn, w_ref[...]) * scale).astype(o_ref.dtype)


def _norm_proj(h, gn, w, scale):
    m, c = h.shape
    tm = TM_PROJ
    n = w.shape[1]
    return pl.pallas_call(
        functools.partial(_norm_proj_kernel, scale=scale),
        out_shape=jax.ShapeDtypeStruct((m, n), BF16),
        grid=(m // tm,),
        in_specs=[pl.BlockSpec((tm, c), lambda i: (i, 0)), _full(gn.shape), _full(w.shape)],
        out_specs=pl.BlockSpec((tm, n), lambda i: (i, 0)),
        compiler_params=_params("parallel"),
        name="norm_proj",
    )(h, gn, w)


MASK_BIAS = -1e30
SOFTPLUS_LINEAR = 100.0
ATTN_SETTLED = 160.0


def _sb_attn_kernel(q_ref, k_ref, v_ref, o_ref, carry_ref, acc_ref, bias_ref, qh_ref, vh_ref,
                    *slot_refs, tq, nq):
    z_ref, e_ref, tot_ref, att_ref = zip(slot_refs[:4], slot_refs[4:])
    tk = tq
    heads = range(HEADS_PER_BLOCK)
    lane = lax.broadcasted_iota(jnp.int32, (1, LANES), 1)
    head0 = lane < HEAD_DIM
    jr = lax.broadcasted_iota(jnp.int32, (tk, tk), 0)
    sc = lax.broadcasted_iota(jnp.int32, (tk, tk), 1)
    suffix_mat = (jr > sc).astype(BF16)
    ones_mat = jnp.ones((tk, LANES), BF16)
    zero = jnp.zeros((), BF16)

    def rows(ref, idx, n):
        return ref[pl.ds(pl.multiple_of(idx * n, n), n), :]

    for ref in slot_refs + (carry_ref, acc_ref):
        ref[...] = jnp.zeros_like(ref)
    bias_ref[0] = jnp.zeros((tq, tk), F32)
    bias_ref[1] = jnp.where(sc < jr, 0.0, MASK_BIAS)
    q = q_ref[0]
    v = v_ref[0]
    seq = nq * tq
    qh_ref[0] = jnp.where(head0, q, zero)
    qh_ref[1] = jnp.where(head0, zero, q)
    vh_ref[0, :seq] = jnp.where(head0, v, zero)
    vh_ref[1, :seq] = jnp.where(head0, zero, v)
    vh_ref[:, seq:] = jnp.zeros((HEADS_PER_BLOCK, tk, LANES), BF16)

    def stage0(t, kv, slot):
        kb = rows(k_ref.at[0], jnp.minimum(kv, nq - 1), tk)
        bias = bias_ref[(kv == t).astype(jnp.int32)]
        for h in heads:
            z_ref[slot][h] = _dot_nt(rows(qh_ref.at[h], t, tq), kb) + bias

    def stage1_head(slot):
        sp16, zs = [], []
        for h in heads:
            z = z_ref[slot][h]
            sp = jnp.maximum(z, jnp.log(1.0 + jnp.exp2(jnp.minimum(z, SOFTPLUS_LINEAR))) * LOG2E)
            sp16.append(sp.astype(BF16))
            zs.append(z - sp)
        return sp16, zs

    def stage1_tail(slot, sp16, zs):
        for h in heads:
            e_ref[slot][h] = zs[h] - _dot(sp16[h], suffix_mat)
            tot_ref[slot][h] = _dot(sp16[h], ones_mat)

    def stage2(t, kv, slot):
        first = kv == t
        low = None
        for h in heads:
            carry = jnp.where(first, 0.0, carry_ref[slot, h])
            new = carry + tot_ref[slot][h]
            carry_ref[slot, h] = new
            low = new if low is None else jnp.minimum(low, new)
            att = jnp.exp2(e_ref[slot][h] - jnp.concatenate([carry] * (tk // LANES), axis=1))
            att_ref[slot][h] = att.astype(BF16)
        return (jnp.min(low) >= ATTN_SETTLED) & (kv != nq)

    def stage3(t, kv, slot):
        pv = (_dot(att_ref[slot][0], rows(vh_ref.at[0], kv, tk))
              + _dot(att_ref[slot][1], rows(vh_ref.at[1], kv, tk)))
        acc = jnp.where(kv == t, 0.0, acc_ref[slot]) + pv
        acc_ref[slot] = acc
        o_ref[0, pl.ds(pl.multiple_of(t * tq, tq), tq), :] = acc.astype(o_ref.dtype)

    def advance(t0, k0, t2, settled, t_end):
        tile_done = (k0 == 0) | (settled & (t0 == t2))
        nt = jnp.where(tile_done, t0 + 1, t0)
        ended = (k0 == nq) | (nt == t_end)
        return (jnp.where(ended, t_end - 1, nt),
                jnp.where(ended, nq, jnp.where(tile_done, nt, k0 - 1)))

    half = nq // 2
    t_end = (half, nq)

    def step(cur, state):
        nxt, (b1, b2, b3), n_ended = state
        old = 1 - cur
        t0, k0 = nxt[cur]
        stage0(t0, k0, cur)
        stage3(*b3, old)
        held = stage1_head(old)
        settled = stage2(*b2, cur)
        stage1_tail(old, *held)
        moved = advance(t0, k0, b2[0], settled, t_end[cur])
        nxt = (moved, nxt[1]) if cur == 0 else (nxt[0], moved)
        ended = (k0 == nq).astype(jnp.int32)
        n_ended = (n_ended[0] + ended, n_ended[1]) if cur == 0 else (n_ended[0], n_ended[1] + ended)
        return nxt, ((t0, k0), b1, b2), n_ended

    def trip(state):
        return step(1, step(0, state))

    i32 = lambda x: jnp.int32(x)
    idle = ((i32(half - 1), i32(nq)), (i32(nq - 1), i32(nq)))
    start = ((i32(0), i32(0)), (i32(half), i32(half)))
    lax.while_loop(lambda s: (s[2][0] < 2) | (s[2][1] < 2), trip,
                   (start, (idle[1], idle[0], idle[1]), (i32(0), i32(0))))


def _sb_attn(q, kv):
    bsz, seq, c = q.shape
    tq = ATTN_TILE
    assert seq % (2 * tq) == 0
    nhb = c // LANES
    qspec = pl.BlockSpec((1, seq, LANES), lambda b, h: (b, 0, h))
    vspec = pl.BlockSpec((1, seq, LANES), lambda b, h: (b, 0, nhb + h))
    blk = (HEADS_PER_BLOCK, tq, tq)
    slot = [pltpu.VMEM(blk, F32), pltpu.VMEM(blk, F32),
            pltpu.VMEM((HEADS_PER_BLOCK, tq, LANES), F32), pltpu.VMEM(blk, BF16)]
    return pl.pallas_call(
        functools.partial(_sb_attn_kernel, tq=tq, nq=seq // tq),
        out_shape=jax.ShapeDtypeStruct((bsz, seq, c), BF16),
        grid=(bsz, nhb),
        in_specs=[qspec, qspec, vspec],
        out_specs=qspec,
        scratch_shapes=[pltpu.VMEM((2, HEADS_PER_BLOCK, tq, LANES), F32), pltpu.VMEM((2, tq, LANES), F32),
                        pltpu.VMEM((2, tq, tq), F32),
                        pltpu.VMEM((HEADS_PER_BLOCK, seq, LANES), BF16),
                        pltpu.VMEM((HEADS_PER_BLOCK, seq + tq, LANES), BF16)] + slot + slot,
        compiler_params=_params("parallel", "parallel"),
        name="sb_attn",
    )(q, kv, kv)


def _router_params(w_group, b_group, w_inner, b_inner):
    c = w_group.shape[0]
    w = jnp.concatenate([w_group, w_inner.reshape(c, N_EXPERTS)], axis=1)
    b = jnp.concatenate([b_group, b_inner.reshape(N_EXPERTS)])
    pad = LANES - w.shape[1]
    return jnp.pad(w, ((0, 0), (0, pad))).T, jnp.pad(b, (0, pad)).reshape(LANES, 1)


def kernel(x, norm_mix, norm_ffn, norm_kv, norm_final, rw_mix, rw_w_rkv, rw_w0, rw_w1, rw_w2, rw_a0, rw_a1, rw_a2, rw_g1, rw_g2, rw_k_k, rw_k_a, rw_r_k, rw_lnx_w, rw_lnx_b, rw_w_out, w_kv, sb_w_q, sb_w_out, moe_w_group, moe_b_group, moe_w_inner, moe_b_inner, moe_w_gate, moe_w_up, moe_w_down):
    bsz, seq, c = x.shape
    depth = norm_mix.shape[0]
    n_a = rw_mix.shape[0]
    m = bsz * seq
    vec = lambda t: t.reshape(1, c)
    bf = lambda t: t.astype(BF16)

    h = x.reshape(m, c)
    kv_sh = None
    for l in range(depth):
        if l < n_a:
            i = l
            r, k, v, lw, kk, a, g = _rwkv_pre(
                h, seq, vec(norm_mix[l]), jnp.pad(rw_mix[i], ((0, 2), (0, 0))), bf(rw_w_rkv[i]),
                vec(rw_w0[i]), bf(rw_w1[i]), bf(rw_w2[i]), vec(rw_a0[i]), bf(rw_a1[i]), bf(rw_a2[i]),
                bf(rw_g1[i]), bf(rw_g2[i]), vec(rw_k_k[i]), vec(rw_k_a[i]))
            s3 = lambda t: t.reshape(bsz, seq, c)
            y = _wkv(s3(r), s3(lw), s3(k), s3(v), s3(kk), s3(a), vec(rw_r_k[i]),
                     vec(rw_lnx_w[i]), vec(rw_lnx_b[i]))
            h = _proj_res(y.reshape(m, c), g, bf(rw_w_out[i]), h)
        else:
            j = l - n_a
            q = _norm_proj(h, vec(norm_mix[l]), bf(sb_w_q[j]), LOG2E * HEAD_DIM ** -0.5)
            o = _sb_attn(q.reshape(bsz, seq, c), kv_sh)
            h = _proj_res(o.reshape(m, c), None, bf(sb_w_out[j]), h)
        w_r, b_r = _router_params(moe_w_group[l], moe_b_group[l], moe_w_inner[l], moe_b_inner[l])
        h = _moe(h, vec(norm_ffn[l]), w_r, b_r, bf(moe_w_gate[l]), bf(moe_w_up[l]), bf(moe_w_down[l]),
                 vec(norm_final), l == depth - 1)
        if l == n_a - 1:
            kv_sh = _norm_proj(h, vec(norm_kv), bf(w_kv), 1.0).reshape(bsz, seq, 2 * c)
    return h.reshape(bsz, seq, c)
```

```python
import functools

import jax
import jax.numpy as jnp
from jax import lax
from jax.experimental import pallas as pl
from jax.experimental.pallas import tpu as pltpu

F32 = jnp.float32
BF16 = jnp.bfloat16

LANES = 128
HEAD_DIM = 64
HEADS_PER_BLOCK = LANES // HEAD_DIM
RMS_EPS = 1e-6
GN_EPS = 64e-5
N_GROUPS = 4
EXPERTS_PER_GROUP = 4
N_EXPERTS = N_GROUPS * EXPERTS_PER_GROUP
WKV_CHUNK = 64
VMEM_LIMIT = 56 * 1024 * 1024
LOG2E = 1.4426950408889634

TM_RWKV_PRE = 512
TM_PROJ = 512
TM_MOE = 1024
WKV_CHUNKS_PER_STEP = 4
WKV_PAIRS_PER_STEP = 8
ATTN_TILE = 2 * LANES


def _params(*sem):
    return pltpu.CompilerParams(dimension_semantics=sem, vmem_limit_bytes=VMEM_LIMIT)


def _dot(a, b):
    return lax.dot_general(a, b, (((1,), (0,)), ((), ())), preferred_element_type=F32)


def _dot_nt(a, b):
    return lax.dot_general(a, b, (((1,), (1,)), ((), ())), preferred_element_type=F32)


def _dot_tn(a, b):
    return lax.dot_general(a, b, (((0,), (0,)), ((), ())), preferred_element_type=F32)


def _bf16_terms(x, n):
    terms = []
    for _ in range(n):
        t = x.astype(BF16)
        terms.append(t)
        x = x - t.astype(F32)
    return terms


def _dot_mask_rhs(x, mask01, n):
    m = mask01.astype(BF16)
    return sum(_dot(t, m) for t in _bf16_terms(x, n))


def _dot_mask_lhs(mask01, x, n):
    m = mask01.astype(BF16)
    return sum(_dot(m, t) for t in _bf16_terms(x, n))


def _dot_3pass_nt(a, b):
    a_hi, a_lo = _bf16_terms(a, 2)
    b_hi, b_lo = _bf16_terms(b, 2)
    return _dot_nt(a_hi, b_hi) + _dot_nt(a_hi, b_lo) + _dot_nt(a_lo, b_hi)


def _rms(x):
    return x * lax.rsqrt(jnp.mean(x * x, axis=-1, keepdims=True) + RMS_EPS)


def _sigmoid(x):
    return 1.0 / (1.0 + jnp.exp(-x))


def _softplus(x):
    return jnp.maximum(x, 0.0) + jnp.log(1.0 + jnp.exp(-jnp.abs(x)))


def _full(shape):
    nd = len(shape)
    return pl.BlockSpec(shape, lambda *_: (0,) * nd)


def _rwkv_pre_kernel(h_ref, hp_ref, gn_ref, mix_ref, wrkv_ref, w0_ref, w1_ref, w2_ref,
                     a0_ref, a1_ref, a2_ref, g1_ref, g2_ref, kk_ref, ka_ref,
                     r_out, k_out, v_out, lw_out, kk_out, a_out, g_out, *, tiles_per_seq):
    i = pl.program_id(0)
    gn = gn_ref[...]
    hn = _rms(h_ref[...]) * gn
    hpn = _rms(hp_ref[7:8, :]) * gn
    hpn = jnp.where(i % tiles_per_seq == 0, 0.0, hpn)
    row = lax.broadcasted_iota(jnp.int32, hn.shape, 0)
    prev = jnp.where(row == 0, hpn, pltpu.roll(hn, 1, axis=0))
    xx = prev - hn

    def mixed(n):
        return (hn + xx * mix_ref[n:n + 1, :]).astype(BF16)

    r = _dot(mixed(0), wrkv_ref[0])
    k = _dot(mixed(1), wrkv_ref[1])
    v = _dot(mixed(2), wrkv_ref[2])
    wl = w0_ref[...] + _dot(jnp.tanh(_dot(mixed(3), w1_ref[...])).astype(BF16), w2_ref[...])
    a = _sigmoid(a0_ref[...] + _dot(_dot(mixed(4), a1_ref[...]).astype(BF16), a2_ref[...]))
    g = _dot(_sigmoid(_dot(mixed(5), g1_ref[...])).astype(BF16), g2_ref[...])
    r_out[...] = r
    k_out[...] = k * (1.0 + (a - 1.0) * ka_ref[...])
    v_out[...] = v
    lw_out[...] = -jnp.exp(-_softplus(-wl) - 0.5)
    kk_out[...] = k * kk_ref[...]
    a_out[...] = a
    g_out[...] = g


def _rwkv_pre(h, seq, gn, mix, wrkv, w0, w1, w2, a0, a1, a2, g1, g2, k_k, k_a):
    m, c = h.shape
    tm = TM_RWKV_PRE
    assert seq % tm == 0 and m % seq == 0
    row = lambda i: (i, 0)
    out = jax.ShapeDtypeStruct((m, c), F32)
    ins = [gn, mix, wrkv, w0, w1, w2, a0, a1, a2, g1, g2, k_k, k_a]
    return pl.pallas_call(
        functools.partial(_rwkv_pre_kernel, tiles_per_seq=seq // tm),
        out_shape=[out] * 7,
        grid=(m // tm,),
        in_specs=[pl.BlockSpec((tm, c), row),
                  pl.BlockSpec((8, c), lambda i: (jnp.maximum(i * (tm // 8) - 1, 0), 0))]
                 + [_full(t.shape) for t in ins],
        out_specs=[pl.BlockSpec((tm, c), row)] * 7,
        compiler_params=_params("parallel"),
        name="rwkv_pre",
    )(h, h, *ins)


def _wkv_kernel(r_ref, lw_ref, k_ref, v_ref, kk_ref, a_ref, rk_ref, gw_ref, gb_ref,
                y_ref, st_ref, *, chunks, pairs):
    L = WKV_CHUNK

    @pl.when(pl.program_id(2) == 0)
    def _():
        st_ref[...] = jnp.zeros_like(st_ref)

    tl = chunks * L
    lane = lax.broadcasted_iota(jnp.int32, (1, LANES), 1)
    head0 = lane < HEAD_DIM
    rr = lax.broadcasted_iota(jnp.int32, (LANES, LANES), 0)
    cc = lax.broadcasted_iota(jnp.int32, (LANES, LANES), 1)
    bd = ((rr < HEAD_DIM) == (cc < HEAD_DIM)).astype(F32)
    eye = rr == cc
    tr = lax.broadcasted_iota(jnp.int32, (tl, tl), 0)
    tc = lax.broadcasted_iota(jnp.int32, (tl, tl), 1)
    shift = L.bit_length() - 1
    same_chunk = (tr >> shift) == (tc >> shift)
    strict = same_chunk & (tr > tc)
    incl = same_chunk & (tr >= tc)
    tri = incl.astype(F32)
    eye_t = (tr == tc).astype(F32)
    m0 = head0.astype(F32)
    masks = (m0, 1.0 - m0)
    head0w = jnp.concatenate([head0, head0], axis=1)
    sel = lambda x0, x1: jnp.where(head0 if x0.shape[1] == LANES else head0w, x0, x1)

    prs = range(pairs)
    heads = range(HEADS_PER_BLOCK)
    units = [(p, h) for p in prs for h in heads]
    lanes = [slice(p * LANES, (p + 1) * LANES) for p in prs]

    r, v, k, at, at16, bt16, kt16, rt, bh16, kh16, v16, cum_end = ([] for _ in range(12))
    for p in prs:
        r_p = r_ref[0, :, lanes[p]]
        lw = lw_ref[0, :, lanes[p]]
        k_p = k_ref[0, :, lanes[p]]
        v_p = v_ref[0, :, lanes[p]]
        kkr = kk_ref[0, :, lanes[p]]
        ag = a_ref[0, :, lanes[p]]
        cum = _dot_mask_lhs(tri, lw, 3)
        ends = [cum[(c + 1) * L - 1:(c + 1) * L, :] for c in range(chunks)]
        cum_l = jnp.concatenate([jnp.broadcast_to(e, (L, LANES)) for e in ends], axis=0)
        w_inv = jnp.exp(-cum)
        w_tail = jnp.exp(cum_l - cum)
        kk = kkr / jnp.maximum(jnp.sqrt(_dot_mask_rhs(kkr * kkr, bd, 1)), 1e-12)
        b = kk * ag
        at_p = -kk * jnp.exp(cum - lw)
        r.append(r_p)
        k.append(k_p)
        v.append(v_p)
        cum_end.append(ends)
        at.append(at_p)
        at16.append(at_p.astype(BF16))
        bt16.append((b * w_inv).astype(BF16))
        kt16.append((k_p * w_inv).astype(BF16))
        rt.append(r_p * jnp.exp(cum))
        bh16.append((b * w_tail).astype(BF16))
        kh16.append((k_p * w_tail).astype(BF16))
        v16.append(v_p.astype(BF16))

    a_ab, a_ak, a_rb, a_rk = [], [], [], []
    for p, h in units:
        lhs = jnp.concatenate([at[p] * masks[h], rt[p] * masks[h]], axis=0).astype(BF16)
        pb = _dot_nt(lhs, bt16[p])
        pk = _dot_nt(lhs, kt16[p])
        a_ab.append(jnp.where(strict, pb[:tl], 0.0))
        a_ak.append(jnp.where(strict, pk[:tl], 0.0).astype(BF16))
        a_rb.append(jnp.where(incl, pb[tl:], 0.0).astype(BF16))
        a_rk.append(jnp.where(incl, pk[tl:], 0.0).astype(BF16))
    def side_by_side(x):
        return sum(x[c * L:(c + 1) * L] for c in range(chunks))

    def block_diag(x16):
        return jnp.where(same_chunk, jnp.concatenate([x16] * chunks, axis=0), jnp.zeros((), BF16))

    n_units = range(len(units))
    eye_w = side_by_side(eye_t)
    t = [eye_w + side_by_side(a) for a in a_ab]
    pd = [a.astype(BF16) for a in a_ab]
    pw = [side_by_side(a).astype(BF16) for a in a_ab]
    akv = [_dot(a_ak[u], v16[units[u][0]]).astype(BF16) for u in n_units]
    rkv = [_dot(a_rk[u], v16[units[u][0]]) for u in n_units]
    for _ in range(L.bit_length() - 2):
        pw = [_dot(pw[u], pd[u]).astype(BF16) for u in n_units]
        pd = [block_diag(pw[u]) for u in n_units]
        t = [t[u] + _dot(t[u].astype(BF16), pd[u]) for u in n_units]
    tau = [_dot(block_diag(t[u].astype(BF16)),
                jnp.concatenate([at16[units[u][0]], akv[u]], axis=1)).astype(BF16) for u in n_units]
    rpy = [_dot(a_rb[u], tau[u]) for u in n_units]

    ta16, u16, rp16, yi = [], [], [], []
    for p in prs:
        u0 = p * HEADS_PER_BLOCK
        tau_p = sel(tau[u0], tau[u0 + 1])
        rpy_p = sel(rpy[u0], rpy[u0 + 1])
        ta16.append(tau_p[:, :LANES])
        u16.append(tau_p[:, LANES:])
        rp16.append((rt[p] + rpy_p[:, :LANES]).astype(BF16))
        yi.append(rpy_p[:, LANES:] + sel(rkv[u0], rkv[u0 + 1]))

    mts, gts = [], []
    for c in range(chunks):
        rows = slice(c * L, (c + 1) * L)
        mts.append([(jnp.where(eye, jnp.exp(cum_end[p][c]), 0.0)
                     + bd * _dot_tn(bh16[p][rows], ta16[p][rows])).astype(BF16) for p in prs])
        gts.append([bd * (_dot_tn(bh16[p][rows], u16[p][rows]) + _dot_tn(kh16[p][rows], v16[p][rows]))
                    for p in prs])
    st = [st_ref[p] for p in prs]
    ys = [[] for _ in prs]
    for c in range(chunks):
        rows = slice(c * L, (c + 1) * L)
        st16 = [s.astype(BF16) for s in st]
        for p in prs:
            ys[p].append(_dot(rp16[p][rows], st16[p]) + yi[p][rows])
        st = [_dot(mts[c][p], st16[p]) + gts[c][p] for p in prs]
    for p in prs:
        st_ref[p] = st[p]
        y = jnp.concatenate(ys[p], axis=0)
        mu = _dot_mask_rhs(y, bd, 1) * (1.0 / HEAD_DIM)
        d = y - mu
        var = _dot_mask_rhs(d * d, bd, 1) * (1.0 / HEAD_DIM)
        yn = d * lax.rsqrt(var + GN_EPS) * gw_ref[:, lanes[p]] + gb_ref[:, lanes[p]]
        bonus = _dot_mask_rhs(r[p] * k[p] * rk_ref[:, lanes[p]], bd, 1) * v[p]
        y_ref[0, :, lanes[p]] = yn + bonus


def _wkv(r, lw, k, v, kk, a, r_k, lnx_w, lnx_b):
    bsz, seq, c = r.shape
    chunks, pairs = WKV_CHUNKS_PER_STEP, WKV_PAIRS_PER_STEP
    tl = chunks * WKV_CHUNK
    width = pairs * LANES
    assert seq % tl == 0 and c % width == 0
    tok = pl.BlockSpec((1, tl, width), lambda b, h, t: (b, t, h))
    par = pl.BlockSpec((1, width), lambda b, h, t: (0, h))
    return pl.pallas_call(
        functools.partial(_wkv_kernel, chunks=chunks, pairs=pairs),
        out_shape=jax.ShapeDtypeStruct((bsz, seq, c), F32),
        grid=(bsz, c // width, seq // tl),
        in_specs=[tok] * 6 + [par] * 3,
        out_specs=tok,
        scratch_shapes=[pltpu.VMEM((pairs, LANES, LANES), F32)],
        compiler_params=_params("parallel", "parallel", "arbitrary"),
        name="wkv",
    )(r, lw, k, v, kk, a, r_k, lnx_w, lnx_b)


def _proj_res_kernel(*refs, gated):
    if gated:
        x_ref, g_ref, w_ref, res_ref, o_ref = refs
        x = (x_ref[...] * g_ref[...]).astype(BF16)
    else:
        x_ref, w_ref, res_ref, o_ref = refs
        x = x_ref[...]
    o_ref[...] = res_ref[...] + _dot(x, w_ref[...])


def _proj_res(x, gate, w, res):
    m, c = res.shape
    tm = TM_PROJ
    row = pl.BlockSpec((tm, c), lambda i: (i, 0))
    gated = gate is not None
    ins = [x, gate, w, res] if gated else [x, w, res]
    specs = [row, row, _full(w.shape), row] if gated else [row, _full(w.shape), row]
    return pl.pallas_call(
        functools.partial(_proj_res_kernel, gated=gated),
        out_shape=jax.ShapeDtypeStruct((m, c), F32),
        grid=(m // tm,),
        in_specs=specs,
        out_specs=row,
        compiler_params=_params("parallel"),
        name="proj_res",
    )(*ins)


def _moe_kernel(h_ref, gn_ref, wr_ref, br_ref, wg_ref, wu_ref, wd_ref, gf_ref, o_ref,
                xn_ref, gates_ref, acc_ref, *, final_norm, d_expert):
    g = pl.program_id(1)
    lane = lax.broadcasted_iota(jnp.int32, (1, LANES), 1)

    @pl.when(g == 0)
    def _():
        xn = _rms(h_ref[...]) * gn_ref[...]
        xn_ref[...] = xn.astype(BF16)
        lt = _dot_3pass_nt(wr_ref[...], xn) + br_ref[...]
        tm = lt.shape[1]
        per = EXPERTS_PER_GROUP

        def max4(x):
            return jnp.maximum(jnp.maximum(x[0], x[1]), jnp.maximum(x[2], x[3]))

        def first_at(x, m):
            return jnp.where(x[0] == m, 0, jnp.where(x[1] == m, 1, jnp.where(x[2] == m, 2, 3)))

        gl = [lt[r:r + 1, :] for r in range(N_GROUPS)]
        gmax = max4(gl)
        gval = 1.0 / sum(jnp.exp(x - gmax) for x in gl)
        gidx = first_at(gl, gmax)
        inner = [[lt[N_GROUPS + gi * per + e:N_GROUPS + gi * per + e + 1, :] for gi in range(N_GROUPS)]
                 for e in range(per)]
        s = [jnp.where(gidx == 0, x[0], jnp.where(gidx == 1, x[1], jnp.where(gidx == 2, x[2], x[3])))
             for x in inner]
        l1 = max4(s)
        i1 = first_at(s, l1)
        s2 = [jnp.where(i1 == e, -jnp.inf, s[e]) for e in range(per)]
        l2 = max4(s2)
        i2 = first_at(s2, l2)
        e2 = jnp.exp(l2 - l1)
        w1 = gval / (1.0 + e2)
        w2 = gval * e2 / (1.0 + e2)
        rid = lax.broadcasted_iota(jnp.int32, (N_EXPERTS, tm), 0)
        gt = (jnp.where(rid == gidx * per + i1, w1, 0.0) + jnp.where(rid == gidx * per + i2, w2, 0.0))
        gt = jnp.concatenate([gt, jnp.zeros((LANES - N_EXPERTS, tm), F32)], axis=0)
        gates_ref[...] = gt.T
        acc_ref[...] = jnp.zeros_like(acc_ref)

    xn = xn_ref[...]
    gates = gates_ref[...]
    first = g * EXPERTS_PER_GROUP
    hm = []
    for e in range(EXPERTS_PER_GROUP):
        hg = _dot(xn, wg_ref[e])
        hu = _dot(xn, wu_ref[e])
        ge = jnp.sum(jnp.where(lane == first + e, gates, 0.0), axis=-1, keepdims=True)
        hm.append(((hg * _sigmoid(hg)) * hu * ge).astype(BF16))
    wd = wd_ref[...].reshape(EXPERTS_PER_GROUP * d_expert, wd_ref.shape[-1])
    acc_ref[...] += _dot(jnp.concatenate(hm, axis=1), wd)

    @pl.when(g == N_GROUPS - 1)
    def _():
        out = h_ref[...] + acc_ref[...]
        if final_norm:
            out = _rms(out) * gf_ref[...]
        o_ref[...] = out


def _moe(h, gn, w_router, b_router, w_gate, w_up, w_down, g_final, final_norm):
    m, c = h.shape
    tm = TM_MOE
    f = w_gate.shape[-1]
    per = EXPERTS_PER_GROUP
    row = pl.BlockSpec((tm, c), lambda i, g: (i, 0))
    return pl.pallas_call(
        functools.partial(_moe_kernel, final_norm=final_norm, d_expert=f),
        out_shape=jax.ShapeDtypeStruct((m, c), F32),
        grid=(m // tm, N_GROUPS),
        in_specs=[row, _full(gn.shape), _full(w_router.shape), _full(b_router.shape),
                  pl.BlockSpec((per, c, f), lambda i, g: (g, 0, 0)),
                  pl.BlockSpec((per, c, f), lambda i, g: (g, 0, 0)),
                  pl.BlockSpec((per, f, c), lambda i, g: (g, 0, 0)),
                  _full(g_final.shape)],
        out_specs=row,
        scratch_shapes=[pltpu.VMEM((tm, c), BF16), pltpu.VMEM((tm, LANES), F32),
                        pltpu.VMEM((tm, c), F32)],
        compiler_params=_params("parallel", "arbitrary"),
        name="moe",
    )(h, gn, w_router, b_router, w_gate, w_up, w_down, g_final)


def _norm_proj_kernel(h_ref, gn_ref, w_ref, o_ref, *, scale):
    xn = (_rms(h_ref[...]) * gn_ref[...]).astype(BF16)
    o_ref[...] = (_dot(xn, w_ref[...]) * scale).astype(o_ref.dtype)


def _norm_proj(h, gn, w, scale):
    m, c = h.shape
    tm = TM_PROJ
    n = w.shape[1]
    return pl.pallas_call(
        functools.partial(_norm_proj_kernel, scale=scale),
        out_shape=jax.ShapeDtypeStruct((m, n), BF16),
        grid=(m // tm,),
        in_specs=[pl.BlockSpec((tm, c), lambda i: (i, 0)), _full(gn.shape), _full(w.shape)],
        out_specs=pl.BlockSpec((tm, n), lambda i: (i, 0)),
        compiler_params=_params("parallel"),
        name="norm_proj",
    )(h, gn, w)


MASK_BIAS = -1e30
SOFTPLUS_LINEAR = 100.0
ATTN_SETTLED = 160.0


def _sb_attn_kernel(q_ref, k_ref, v_ref, o_ref, carry_ref, acc_ref, bias_ref, qh_ref, vh_ref,
                    *slot_refs, tq, nq):
    z_ref, e_ref, tot_ref, att_ref = zip(slot_refs[:4], slot_refs[4:])
    tk = tq
    heads = range(HEADS_PER_BLOCK)
    lane = lax.broadcasted_iota(jnp.int32, (1, LANES), 1)
    head0 = lane < HEAD_DIM
    jr = lax.broadcasted_iota(jnp.int32, (tk, tk), 0)
    sc = lax.broadcasted_iota(jnp.int32, (tk, tk), 1)
    suffix_mat = (jr > sc).astype(BF16)
    ones_mat = jnp.ones((tk, LANES), BF16)
    zero = jnp.zeros((), BF16)

    def rows(ref, idx, n):
        return ref[pl.ds(pl.multiple_of(idx * n, n), n), :]

    for ref in slot_refs + (carry_ref, acc_ref):
        ref[...] = jnp.zeros_like(ref)
    bias_ref[0] = jnp.zeros((tq, tk), F32)
    bias_ref[1] = jnp.where(sc < jr, 0.0, MASK_BIAS)
    q = q_ref[0]
    v = v_ref[0]
    seq = nq * tq
    qh_ref[0] = jnp.where(head0, q, zero)
    qh_ref[1] = jnp.where(head0, zero, q)
    vh_ref[0, :seq] = jnp.where(head0, v, zero)
    vh_ref[1, :seq] = jnp.where(head0, zero, v)
    vh_ref[:, seq:] = jnp.zeros((HEADS_PER_BLOCK, tk, LANES), BF16)

    def stage0(t, kv, slot):
        kb = rows(k_ref.at[0], jnp.minimum(kv, nq - 1), tk)
        bias = bias_ref[(kv == t).astype(jnp.int32)]
        for h in heads:
            z_ref[slot][h] = _dot_nt(rows(qh_ref.at[h], t, tq), kb) + bias

    def stage1_head(slot):
        sp16, zs = [], []
        for h in heads:
            z = z_ref[slot][h]
            sp = jnp.maximum(z, jnp.log(1.0 + jnp.exp2(jnp.minimum(z, SOFTPLUS_LINEAR))) * LOG2E)
            sp16.append(sp.astype(BF16))
            zs.append(z - sp)
        return sp16, zs

    def stage1_tail(slot, sp16, zs):
        for h in heads:
            e_ref[slot][h] = zs[h] - _dot(sp16[h], suffix_mat)
            tot_ref[slot][h] = _dot(sp16[h], ones_mat)

    def stage2(t, kv, slot):
        first = kv == t
        low = None
        for h in heads:
            carry = jnp.where(first, 0.0, carry_ref[slot, h])
            new = carry + tot_ref[slot][h]
            carry_ref[slot, h] = new
            low = new if low is None else jnp.minimum(low, new)
            att = jnp.exp2(e_ref[slot][h] - jnp.concatenate([carry] * (tk // LANES), axis=1))
            att_ref[slot][h] = att.astype(BF16)
        return (jnp.min(low) >= ATTN_SETTLED) & (kv != nq)

    def stage3(t, kv, slot):
        pv = (_dot(att_ref[slot][0], rows(vh_ref.at[0], kv, tk))
              + _dot(att_ref[slot][1], rows(vh_ref.at[1], kv, tk)))
        acc = jnp.where(kv == t, 0.0, acc_ref[slot]) + pv
        acc_ref[slot] = acc
        o_ref[0, pl.ds(pl.multiple_of(t * tq, tq), tq), :] = acc.astype(o_ref.dtype)

    def advance(t0, k0, t2, settled, t_end):
        tile_done = (k0 == 0) | (settled & (t0 == t2))
        nt = jnp.where(tile_done, t0 + 1, t0)
        ended = (k0 == nq) | (nt == t_end)
        return (jnp.where(ended, t_end - 1, nt),
                jnp.where(ended, nq, jnp.where(tile_done, nt, k0 - 1)))

    half = nq // 2
    t_end = (half, nq)

    def step(cur, state):
        nxt, (b1, b2, b3), n_ended = state
        old = 1 - cur
        t0, k0 = nxt[cur]
        stage0(t0, k0, cur)
        stage3(*b3, old)
        held = stage1_head(old)
        settled = stage2(*b2, cur)
        stage1_tail(old, *held)
        moved = advance(t0, k0, b2[0], settled, t_end[cur])
        nxt = (moved, nxt[1]) if cur == 0 else (nxt[0], moved)
        ended = (k0 == nq).astype(jnp.int32)
        n_ended = (n_ended[0] + ended, n_ended[1]) if cur == 0 else (n_ended[0], n_ended[1] + ended)
        return nxt, ((t0, k0), b1, b2), n_ended

    def trip(state):
        return step(1, step(0, state))

    i32 = lambda x: jnp.int32(x)
    idle = ((i32(half - 1), i32(nq)), (i32(nq - 1), i32(nq)))
    start = ((i32(0), i32(0)), (i32(half), i32(half)))
    lax.while_loop(lambda s: (s[2][0] < 2) | (s[2][1] < 2), trip,
                   (start, (idle[1], idle[0], idle[1]), (i32(0), i32(0))))


def _sb_attn(q, kv):
    bsz, seq, c = q.shape
    tq = ATTN_TILE
    assert seq % (2 * tq) == 0
    nhb = c // LANES
    qspec = pl.BlockSpec((1, seq, LANES), lambda b, h: (b, 0, h))
    vspec = pl.BlockSpec((1, seq, LANES), lambda b, h: (b, 0, nhb + h))
    blk = (HEADS_PER_BLOCK, tq, tq)
    slot = [pltpu.VMEM(blk, F32), pltpu.VMEM(blk, F32),
            pltpu.VMEM((HEADS_PER_BLOCK, tq, LANES), F32), pltpu.VMEM(blk, BF16)]
    return pl.pallas_call(
        functools.partial(_sb_attn_kernel, tq=tq, nq=seq // tq),
        out_shape=jax.ShapeDtypeStruct((bsz, seq, c), BF16),
        grid=(bsz, nhb),
        in_specs=[qspec, qspec, vspec],
        out_specs=qspec,
        scratch_shapes=[pltpu.VMEM((2, HEADS_PER_BLOCK, tq, LANES), F32), pltpu.VMEM((2, tq, LANES), F32),
                        pltpu.VMEM((2, tq, tq), F32),
                        pltpu.VMEM((HEADS_PER_BLOCK, seq, LANES), BF16),
                        pltpu.VMEM((HEADS_PER_BLOCK, seq + tq, LANES), BF16)] + slot + slot,
        compiler_params=_params("parallel", "parallel"),
        name="sb_attn",
    )(q, kv, kv)


def _router_params(w_group, b_group, w_inner, b_inner):
    c = w_group.shape[0]
    w = jnp.concatenate([w_group, w_inner.reshape(c, N_EXPERTS)], axis=1)
    b = jnp.concatenate([b_group, b_inner.reshape(N_EXPERTS)])
    pad = LANES - w.shape[1]
    return jnp.pad(w, ((0, 0), (0, pad))).T, jnp.pad(b, (0, pad)).reshape(LANES, 1)


def kernel(x, norm_mix, norm_ffn, norm_kv, norm_final, rw_mix, rw_w_rkv, rw_w0, rw_w1, rw_w2, rw_a0, rw_a1, rw_a2, rw_g1, rw_g2, rw_k_k, rw_k_a, rw_r_k, rw_lnx_w, rw_lnx_b, rw_w_out, w_kv, sb_w_q, sb_w_out, moe_w_group, moe_b_group, moe_w_inner, moe_b_inner, moe_w_gate, moe_w_up, moe_w_down):
    bsz, seq, c = x.shape
    depth = norm_mix.shape[0]
    n_a = rw_mix.shape[0]
    m = bsz * seq
    vec = lambda t: t.reshape(1, c)
    bf = lambda t: t.astype(BF16)

    h = x.reshape(m, c)
    kv_sh = None
    for l in range(depth):
        if l < n_a:
            i = l
            r, k, v, lw, kk, a, g = _rwkv_pre(
                h, seq, vec(norm_mix[l]), jnp.pad(rw_mix[i], ((0, 2), (0, 0))), bf(rw_w_rkv[i]),
                vec(rw_w0[i]), bf(rw_w1[i]), bf(rw_w2[i]), vec(rw_a0[i]), bf(rw_a1[i]), bf(rw_a2[i]),
                bf(rw_g1[i]), bf(rw_g2[i]), vec(rw_k_k[i]), vec(rw_k_a[i]))
            s3 = lambda t: t.reshape(bsz, seq, c)
            y = _wkv(s3(r), s3(lw), s3(k), s3(v), s3(kk), s3(a), vec(rw_r_k[i]),
                     vec(rw_lnx_w[i]), vec(rw_lnx_b[i]))
            h = _proj_res(y.reshape(m, c), g, bf(rw_w_out[i]), h)
        else:
            j = l - n_a
            q = _norm_proj(h, vec(norm_mix[l]), bf(sb_w_q[j]), LOG2E * HEAD_DIM ** -0.5)
            o = _sb_attn(q.reshape(bsz, seq, c), kv_sh)
            h = _proj_res(o.reshape(m, c), None, bf(sb_w_out[j]), h)
        w_r, b_r = _router_params(moe_w_group[l], moe_b_group[l], moe_w_inner[l], moe_b_inner[l])
        h = _moe(h, vec(norm_ffn[l]), w_r, b_r, bf(moe_w_gate[l]), bf(moe_w_up[l]), bf(moe_w_down[l]),
                 vec(norm_final), l == depth - 1)
        if l == n_a - 1:
            kv_sh = _norm_proj(h, vec(norm_kv), bf(w_kv), 1.0).reshape(bsz, seq, 2 * c)
    return h.reshape(bsz, seq, c)
```

```python
import functools

import jax
import jax.numpy as jnp
from jax import lax
from jax.experimental import pallas as pl
from jax.experimental.pallas import tpu as pltpu

F32 = jnp.float32
BF16 = jnp.bfloat16

LANES = 128
HEAD_DIM = 64
HEADS_PER_BLOCK = LANES // HEAD_DIM
RMS_EPS = 1e-6
GN_EPS = 64e-5
N_GROUPS = 4
EXPERTS_PER_GROUP = 4
N_EXPERTS = N_GROUPS * EXPERTS_PER_GROUP
WKV_CHUNK = 64
VMEM_LIMIT = 56 * 1024 * 1024
LOG2E = 1.4426950408889634

TM_RWKV_PRE = 512
TM_PROJ = 512
TM_MOE = 1024
WKV_CHUNKS_PER_STEP = 4
WKV_PAIRS_PER_STEP = 8
ATTN_TILE = 2 * LANES


def _params(*sem):
    return pltpu.CompilerParams(dimension_semantics=sem, vmem_limit_bytes=VMEM_LIMIT)


def _dot(a, b):
    return lax.dot_general(a, b, (((1,), (0,)), ((), ())), preferred_element_type=F32)


def _dot_nt(a, b):
    return lax.dot_general(a, b, (((1,), (1,)), ((), ())), preferred_element_type=F32)


def _dot_tn(a, b):
    return lax.dot_general(a, b, (((0,), (0,)), ((), ())), preferred_element_type=F32)


def _bf16_terms(x, n):
    terms = []
    for _ in range(n):
        t = x.astype(BF16)
        terms.append(t)
        x = x - t.astype(F32)
    return terms


def _dot_mask_rhs(x, mask01, n):
    m = mask01.astype(BF16)
    return sum(_dot(t, m) for t in _bf16_terms(x, n))


def _dot_mask_lhs(mask01, x, n):
    m = mask01.astype(BF16)
    return sum(_dot(m, t) for t in _bf16_terms(x, n))


def _dot_3pass_nt(a, b):
    a_hi, a_lo = _bf16_terms(a, 2)
    b_hi, b_lo = _bf16_terms(b, 2)
    return _dot_nt(a_hi, b_hi) + _dot_nt(a_hi, b_lo) + _dot_nt(a_lo, b_hi)


def _rms(x):
    return x * lax.rsqrt(jnp.mean(x * x, axis=-1, keepdims=True) + RMS_EPS)


def _sigmoid(x):
    return 1.0 / (1.0 + jnp.exp(-x))


def _softplus(x):
    return jnp.maximum(x, 0.0) + jnp.log(1.0 + jnp.exp(-jnp.abs(x)))


def _full(shape):
    nd = len(shape)
    return pl.BlockSpec(shape, lambda *_: (0,) * nd)


def _rwkv_pre_kernel(h_ref, hp_ref, gn_ref, mix_ref, wrkv_ref, w0_ref, w1_ref, w2_ref,
                     a0_ref, a1_ref, a2_ref, g1_ref, g2_ref, kk_ref, ka_ref,
                     r_out, k_out, v_out, lw_out, kk_out, a_out, g_out, *, tiles_per_seq):
    i = pl.program_id(0)
    gn = gn_ref[...]
    hn = _rms(h_ref[...]) * gn
    hpn = _rms(hp_ref[7:8, :]) * gn
    hpn = jnp.where(i % tiles_per_seq == 0, 0.0, hpn)
    row = lax.broadcasted_iota(jnp.int32, hn.shape, 0)
    prev = jnp.where(row == 0, hpn, pltpu.roll(hn, 1, axis=0))
    xx = prev - hn

    def mixed(n):
        return (hn + xx * mix_ref[n:n + 1, :]).astype(BF16)

    r = _dot(mixed(0), wrkv_ref[0])
    k = _dot(mixed(1), wrkv_ref[1])
    v = _dot(mixed(2), wrkv_ref[2])
    wl = w0_ref[...] + _dot(jnp.tanh(_dot(mixed(3), w1_ref[...])).astype(BF16), w2_ref[...])
    a = _sigmoid(a0_ref[...] + _dot(_dot(mixed(4), a1_ref[...]).astype(BF16), a2_ref[...]))
    g = _dot(_sigmoid(_dot(mixed(5), g1_ref[...])).astype(BF16), g2_ref[...])
    r_out[...] = r
    k_out[...] = k * (1.0 + (a - 1.0) * ka_ref[...])
    v_out[...] = v
    lw_out[...] = -jnp.exp(-_softplus(-wl) - 0.5)
    kk_out[...] = k * kk_ref[...]
    a_out[...] = a
    g_out[...] = g


def _rwkv_pre(h, seq, gn, mix, wrkv, w0, w1, w2, a0, a1, a2, g1, g2, k_k, k_a):
    m, c = h.shape
    tm = TM_RWKV_PRE
    assert seq % tm == 0 and m % seq == 0
    row = lambda i: (i, 0)
    out = jax.ShapeDtypeStruct((m, c), F32)
    ins = [gn, mix, wrkv, w0, w1, w2, a0, a1, a2, g1, g2, k_k, k_a]
    return pl.pallas_call(
        functools.partial(_rwkv_pre_kernel, tiles_per_seq=seq // tm),
        out_shape=[out] * 7,
        grid=(m // tm,),
        in_specs=[pl.BlockSpec((tm, c), row),
                  pl.BlockSpec((8, c), lambda i: (jnp.maximum(i * (tm // 8) - 1, 0), 0))]
                 + [_full(t.shape) for t in ins],
        out_specs=[pl.BlockSpec((tm, c), row)] * 7,
        compiler_params=_params("parallel"),
        name="rwkv_pre",
    )(h, h, *ins)


def _wkv_kernel(r_ref, lw_ref, k_ref, v_ref, kk_ref, a_ref, rk_ref, gw_ref, gb_ref,
                y_ref, st_ref, *, chunks, pairs):
    L = WKV_CHUNK

    @pl.when(pl.program_id(2) == 0)
    def _():
        st_ref[...] = jnp.zeros_like(st_ref)

    tl = chunks * L
    lane = lax.broadcasted_iota(jnp.int32, (1, LANES), 1)
    head0 = lane < HEAD_DIM
    rr = lax.broadcasted_iota(jnp.int32, (LANES, LANES), 0)
    cc = lax.broadcasted_iota(jnp.int32, (LANES, LANES), 1)
    bd = ((rr < HEAD_DIM) == (cc < HEAD_DIM)).astype(F32)
    eye = rr == cc
    tr = lax.broadcasted_iota(jnp.int32, (tl, tl), 0)
    tc = lax.broadcasted_iota(jnp.int32, (tl, tl), 1)
    shift = L.bit_length() - 1
    same_chunk = (tr >> shift) == (tc >> shift)
    strict = same_chunk & (tr > tc)
    incl = same_chunk & (tr >= tc)
    tri = incl.astype(F32)
    eye_t = (tr == tc).astype(F32)
    m0 = head0.astype(F32)
    masks = (m0, 1.0 - m0)
    head0w = jnp.concatenate([head0, head0], axis=1)
    sel = lambda x0, x1: jnp.where(head0 if x0.shape[1] == LANES else head0w, x0, x1)

    prs = range(pairs)
    heads = range(HEADS_PER_BLOCK)
    units = [(p, h) for p in prs for h in heads]
    lanes = [slice(p * LANES, (p + 1) * LANES) for p in prs]

    r, v, k, at, at16, bt16, kt16, rt, bh16, kh16, v16, cum_end = ([] for _ in range(12))
    for p in prs:
        r_p = r_ref[0, :, lanes[p]]
        lw = lw_ref[0, :, lanes[p]]
        k_p = k_ref[0, :, lanes[p]]
        v_p = v_ref[0, :, lanes[p]]
        kkr = kk_ref[0, :, lanes[p]]
        ag = a_ref[0, :, lanes[p]]
        cum = _dot_mask_lhs(tri, lw, 3)
        ends = [cum[(c + 1) * L - 1:(c + 1) * L, :] for c in range(chunks)]
        cum_l = jnp.concatenate([jnp.broadcast_to(e, (L, LANES)) for e in ends], axis=0)
        w_inv = jnp.exp(-cum)
        w_tail = jnp.exp(cum_l - cum)
        kk = kkr / jnp.maximum(jnp.sqrt(_dot_mask_rhs(kkr * kkr, bd, 1)), 1e-12)
        b = kk * ag
        at_p = -kk * jnp.exp(cum - lw)
        r.append(r_p)
        k.append(k_p)
        v.append(v_p)
        cum_end.append(ends)
        at.append(at_p)
        at16.append(at_p.astype(BF16))
        bt16.append((b * w_inv).astype(BF16))
        kt16.append((k_p * w_inv).astype(BF16))
        rt.append(r_p * jnp.exp(cum))
        bh16.append((b * w_tail).astype(BF16))
        kh16.append((k_p * w_tail).astype(BF16))
        v16.append(v_p.astype(BF16))

    a_ab, a_ak, a_rb, a_rk = [], [], [], []
    for p, h in units:
        lhs = jnp.concatenate([at[p] * masks[h], rt[p] * masks[h]], axis=0).astype(BF16)
        pb = _dot_nt(lhs, bt16[p])
        pk = _dot_nt(lhs, kt16[p])
        a_ab.append(jnp.where(strict, pb[:tl], 0.0))
        a_ak.append(jnp.where(strict, pk[:tl], 0.0).astype(BF16))
        a_rb.append(jnp.where(incl, pb[tl:], 0.0).astype(BF16))
        a_rk.append(jnp.where(incl, pk[tl:], 0.0).astype(BF16))
    def side_by_side(x):
        return sum(x[c * L:(c + 1) * L] for c in range(chunks))

    def block_diag(x16):
        return jnp.where(same_chunk, jnp.concatenate([x16] * chunks, axis=0), jnp.zeros((), BF16))

    n_units = range(len(units))
    eye_w = side_by_side(eye_t)
    t = [eye_w + side_by_side(a) for a in a_ab]
    pd = [a.astype(BF16) for a in a_ab]
    pw = [side_by_side(a).astype(BF16) for a in a_ab]
    akv = [_dot(a_ak[u], v16[units[u][0]]).astype(BF16) for u in n_units]
    rkv = [_dot(a_rk[u], v16[units[u][0]]) for u in n_units]
    for _ in range(L.bit_length() - 2):
        pw = [_dot(pw[u], pd[u]).astype(BF16) for u in n_units]
        pd = [block_diag(pw[u]) for u in n_units]
        t = [t[u] + _dot(t[u].astype(BF16), pd[u]) for u in n_units]
    tau = [_dot(block_diag(t[u].astype(BF16)),
                jnp.concatenate([at16[units[u][0]], akv[u]], axis=1)).astype(BF16) for u in n_units]
    rpy = [_dot(a_rb[u], tau[u]) for u in n_units]

    ta16, u16, rp16, yi = [], [], [], []
    for p in prs:
        u0 = p * HEADS_PER_BLOCK
        tau_p = sel(tau[u0], tau[u0 + 1])
        rpy_p = sel(rpy[u0], rpy[u0 + 1])
        ta16.append(tau_p[:, :LANES])
        u16.append(tau_p[:, LANES:])
        rp16.append((rt[p] + rpy_p[:, :LANES]).astype(BF16))
        yi.append(rpy_p[:, LANES:] + sel(rkv[u0], rkv[u0 + 1]))

    mts, gts = [], []
    for c in range(chunks):
        rows = slice(c * L, (c + 1) * L)
        mts.append([(jnp.where(eye, jnp.exp(cum_end[p][c]), 0.0)
                     + bd * _dot_tn(bh16[p][rows], ta16[p][rows])).astype(BF16) for p in prs])
        gts.append([bd * (_dot_tn(bh16[p][rows], u16[p][rows]) + _dot_tn(kh16[p][rows], v16[p][rows]))
                    for p in prs])
    st = [st_ref[p] for p in prs]
    ys = [[] for _ in prs]
    for c in range(chunks):
        rows = slice(c * L, (c + 1) * L)
        st16 = [s.astype(BF16) for s in st]
        for p in prs:
            ys[p].append(_dot(rp16[p][rows], st16[p]) + yi[p][rows])
        st = [_dot(mts[c][p], st16[p]) + gts[c][p] for p in prs]
    for p in prs:
        st_ref[p] = st[p]
        y = jnp.concatenate(ys[p], axis=0)
        mu = _dot_mask_rhs(y, bd, 1) * (1.0 / HEAD_DIM)
        d = y - mu
        var = _dot_mask_rhs(d * d, bd, 1) * (1.0 / HEAD_DIM)
        yn = d * lax.rsqrt(var + GN_EPS) * gw_ref[:, lanes[p]] + gb_ref[:, lanes[p]]
        bonus = _dot_mask_rhs(r[p] * k[p] * rk_ref[:, lanes[p]], bd, 1) * v[p]
        y_ref[0, :, lanes[p]] = yn + bonus


def _wkv(r, lw, k, v, kk, a, r_k, lnx_w, lnx_b):
    bsz, seq, c = r.shape
    chunks, pairs = WKV_CHUNKS_PER_STEP, WKV_PAIRS_PER_STEP
    tl = chunks * WKV_CHUNK
    width = pairs * LANES
    assert seq % tl == 0 and c % width == 0
    tok = pl.BlockSpec((1, tl, width), lambda b, h, t: (b, t, h))
    par = pl.BlockSpec((1, width), lambda b, h, t: (0, h))
    return pl.pallas_call(
        functools.partial(_wkv_kernel, chunks=chunks, pairs=pairs),
        out_shape=jax.ShapeDtypeStruct((bsz, seq, c), F32),
        grid=(bsz, c // width, seq // tl),
        in_specs=[tok] * 6 + [par] * 3,
        out_specs=tok,
        scratch_shapes=[pltpu.VMEM((pairs, LANES, LANES), F32)],
        compiler_params=_params("parallel", "parallel", "arbitrary"),
        name="wkv",
    )(r, lw, k, v, kk, a, r_k, lnx_w, lnx_b)


def _proj_res_kernel(*refs, gated):
    if gated:
        x_ref, g_ref, w_ref, res_ref, o_ref = refs
        x = (x_ref[...] * g_ref[...]).astype(BF16)
    else:
        x_ref, w_ref, res_ref, o_ref = refs
        x = x_ref[...]
    o_ref[...] = res_ref[...] + _dot(x, w_ref[...])


def _proj_res(x, gate, w, res):
    m, c = res.shape
    tm = TM_PROJ
    row = pl.BlockSpec((tm, c), lambda i: (i, 0))
    gated = gate is not None
    ins = [x, gate, w, res] if gated else [x, w, res]
    specs = [row, row, _full(w.shape), row] if gated else [row, _full(w.shape), row]
    return pl.pallas_call(
        functools.partial(_proj_res_kernel, gated=gated),
        out_shape=jax.ShapeDtypeStruct((m, c), F32),
        grid=(m // tm,),
        in_specs=specs,
        out_specs=row,
        compiler_params=_params("parallel"),
        name="proj_res",
    )(*ins)


def _moe_kernel(h_ref, gn_ref, wr_ref, br_ref, wg_ref, wu_ref, wd_ref, gf_ref, o_ref,
                xn_ref, gates_ref, acc_ref, *, final_norm, d_expert):
    g = pl.program_id(1)
    lane = lax.broadcasted_iota(jnp.int32, (1, LANES), 1)

    @pl.when(g == 0)
    def _():
        xn = _rms(h_ref[...]) * gn_ref[...]
        xn_ref[...] = xn.astype(BF16)
        lt = _dot_3pass_nt(wr_ref[...], xn) + br_ref[...]
        tm = lt.shape[1]
        per = EXPERTS_PER_GROUP

        def max4(x):
            return jnp.maximum(jnp.maximum(x[0], x[1]), jnp.maximum(x[2], x[3]))

        def first_at(x, m):
            return jnp.where(x[0] == m, 0, jnp.where(x[1] == m, 1, jnp.where(x[2] == m, 2, 3)))

        gl = [lt[r:r + 1, :] for r in range(N_GROUPS)]
        gmax = max4(gl)
        gval = 1.0 / sum(jnp.exp(x - gmax) for x in gl)
        gidx = first_at(gl, gmax)
        inner = [[lt[N_GROUPS + gi * per + e:N_GROUPS + gi * per + e + 1, :] for gi in range(N_GROUPS)]
                 for e in range(per)]
        s = [jnp.where(gidx == 0, x[0], jnp.where(gidx == 1, x[1], jnp.where(gidx == 2, x[2], x[3])))
             for x in inner]
        l1 = max4(s)
        i1 = first_at(s, l1)
        s2 = [jnp.where(i1 == e, -jnp.inf, s[e]) for e in range(per)]
        l2 = max4(s2)
        i2 = first_at(s2, l2)
        e2 = jnp.exp(l2 - l1)
        w1 = gval / (1.0 + e2)
        w2 = gval * e2 / (1.0 + e2)
        rid = lax.broadcasted_iota(jnp.int32, (N_EXPERTS, tm), 0)
        gt = (jnp.where(rid == gidx * per + i1, w1, 0.0) + jnp.where(rid == gidx * per + i2, w2, 0.0))
        gt = jnp.concatenate([gt, jnp.zeros((LANES - N_EXPERTS, tm), F32)], axis=0)
        gates_ref[...] = gt.T
        acc_ref[...] = jnp.zeros_like(acc_ref)

    xn = xn_ref[...]
    gates = gates_ref[...]
    first = g * EXPERTS_PER_GROUP
    hm = []
    for e in range(EXPERTS_PER_GROUP):
        hg = _dot(xn, wg_ref[e])
        hu = _dot(xn, wu_ref[e])
        ge = jnp.sum(jnp.where(lane == first + e, gates, 0.0), axis=-1, keepdims=True)
        hm.append(((hg * _sigmoid(hg)) * hu * ge).astype(BF16))
    wd = wd_ref[...].reshape(EXPERTS_PER_GROUP * d_expert, wd_ref.shape[-1])
    acc_ref[...] += _dot(jnp.concatenate(hm, axis=1), wd)

    @pl.when(g == N_GROUPS - 1)
    def _():
        out = h_ref[...] + acc_ref[...]
        if final_norm:
            out = _rms(out) * gf_ref[...]
        o_ref[...] = out


def _moe(h, gn, w_router, b_router, w_gate, w_up, w_down, g_final, final_norm):
    m, c = h.shape
    tm = TM_MOE
    f = w_gate.shape[-1]
    per = EXPERTS_PER_GROUP
    row = pl.BlockSpec((tm, c), lambda i, g: (i, 0))
    return pl.pallas_call(
        functools.partial(_moe_kernel, final_norm=final_norm, d_expert=f),
        out_shape=jax.ShapeDtypeStruct((m, c), F32),
        grid=(m // tm, N_GROUPS),
        in_specs=[row, _full(gn.shape), _full(w_router.shape), _full(b_router.shape),
                  pl.BlockSpec((per, c, f), lambda i, g: (g, 0, 0)),
                  pl.BlockSpec((per, c, f), lambda i, g: (g, 0, 0)),
                  pl.BlockSpec((per, f, c), lambda i, g: (g, 0, 0)),
                  _full(g_final.shape)],
        out_specs=row,
        scratch_shapes=[pltpu.VMEM((tm, c), BF16), pltpu.VMEM((tm, LANES), F32),
                        pltpu.VMEM((tm, c), F32)],
        compiler_params=_params("parallel", "arbitrary"),
        name="moe",
    )(h, gn, w_router, b_router, w_gate, w_up, w_down, g_final)


def _norm_proj_kernel(h_ref, *refs, scales):
    n = len(scales)
    gn_refs, w_refs, o_refs = refs[:n], refs[n:2 * n], refs[2 * n:]
    y = _rms(h_ref[...])
    for gn_ref, w_ref, o_ref, scale in zip(gn_refs, w_refs, o_refs, scales):
        xn = (y * gn_ref[...]).astype(BF16)
        o_ref[...] = (_dot(xn, w_ref[...]) * scale).astype(o_ref.dtype)


def _norm_proj(h, gains, weights, scales):
    m, c = h.shape
    tm = TM_PROJ
    row = lambda n: pl.BlockSpec((tm, n), lambda i: (i, 0))
    return pl.pallas_call(
        functools.partial(_norm_proj_kernel, scales=tuple(scales)),
        out_shape=[jax.ShapeDtypeStruct((m, w.shape[1]), BF16) for w in weights],
        grid=(m // tm,),
        in_specs=[row(c)] + [_full(g.shape) for g in gains] + [_full(w.shape) for w in weights],
        out_specs=[row(w.shape[1]) for w in weights],
        compiler_params=_params("parallel"),
        name="norm_proj",
    )(h, *gains, *weights)


MASK_BIAS = -1e30
SOFTPLUS_LINEAR = 100.0
ATTN_SETTLED = 160.0


def _sb_attn_kernel(q_ref, k_ref, v_ref, o_ref, carry_ref, acc_ref, bias_ref, qh_ref, vh_ref,
                    *slot_refs, tq, nq):
    z_ref, e_ref, tot_ref, att_ref = zip(slot_refs[:4], slot_refs[4:])
    tk = tq
    heads = range(HEADS_PER_BLOCK)
    lane = lax.broadcasted_iota(jnp.int32, (1, LANES), 1)
    head0 = lane < HEAD_DIM
    jr = lax.broadcasted_iota(jnp.int32, (tk, tk), 0)
    sc = lax.broadcasted_iota(jnp.int32, (tk, tk), 1)
    suffix_mat = (jr > sc).astype(BF16)
    ones_mat = jnp.ones((tk, LANES), BF16)
    zero = jnp.zeros((), BF16)

    def rows(ref, idx, n):
        return ref[pl.ds(pl.multiple_of(idx * n, n), n), :]

    for ref in slot_refs + (carry_ref, acc_ref):
        ref[...] = jnp.zeros_like(ref)
    bias_ref[0] = jnp.zeros((tq, tk), F32)
    bias_ref[1] = jnp.where(sc < jr, 0.0, MASK_BIAS)
    q = q_ref[0]
    v = v_ref[0]
    seq = nq * tq
    qh_ref[0] = jnp.where(head0, q, zero)
    qh_ref[1] = jnp.where(head0, zero, q)
    vh_ref[0, :seq] = jnp.where(head0, v, zero)
    vh_ref[1, :seq] = jnp.where(head0, zero, v)
    vh_ref[:, seq:] = jnp.zeros((HEADS_PER_BLOCK, tk, LANES), BF16)

    def stage0(t, kv, slot):
        kb = rows(k_ref.at[0], jnp.minimum(kv, nq - 1), tk)
        bias = bias_ref[(kv == t).astype(jnp.int32)]
        for h in heads:
            z_ref[slot][h] = _dot_nt(rows(qh_ref.at[h], t, tq), kb) + bias

    def stage1_head(slot):
        sp16, zs = [], []
        for h in heads:
            z = z_ref[slot][h]
            sp = jnp.maximum(z, jnp.log(1.0 + jnp.exp2(jnp.minimum(z, SOFTPLUS_LINEAR))) * LOG2E)
            sp16.append(sp.astype(BF16))
            zs.append(z - sp)
        return sp16, zs

    def stage1_tail(slot, sp16, zs):
        for h in heads:
            e_ref[slot][h] = zs[h] - _dot(sp16[h], suffix_mat)
            tot_ref[slot][h] = _dot(sp16[h], ones_mat)

    def stage2(t, kv, slot):
        first = kv == t
        low = None
        for h in heads:
            carry = jnp.where(first, 0.0, carry_ref[slot, h])
            new = carry + tot_ref[slot][h]
            carry_ref[slot, h] = new
            low = new if low is None else jnp.minimum(low, new)
            att = jnp.exp2(e_ref[slot][h] - jnp.concatenate([carry] * (tk // LANES), axis=1))
            att_ref[slot][h] = att.astype(BF16)
        return (jnp.min(low) >= ATTN_SETTLED) & (kv != nq)

    def stage3(t, kv, slot):
        pv = (_dot(att_ref[slot][0], rows(vh_ref.at[0], kv, tk))
              + _dot(att_ref[slot][1], rows(vh_ref.at[1], kv, tk)))
        acc = jnp.where(kv == t, 0.0, acc_ref[slot]) + pv
        acc_ref[slot] = acc
        o_ref[0, pl.ds(pl.multiple_of(t * tq, tq), tq), :] = acc.astype(o_ref.dtype)

    def advance(t0, k0, t2, settled, t_end):
        tile_done = (k0 == 0) | (settled & (t0 == t2))
        nt = jnp.where(tile_done, t0 + 1, t0)
        ended = (k0 == nq) | (nt == t_end)
        return (jnp.where(ended, t_end - 1, nt),
                jnp.where(ended, nq, jnp.where(tile_done, nt, k0 - 1)))

    half = nq // 2
    t_end = (half, nq)

    def step(cur, state):
        nxt, (b1, b2, b3), n_ended = state
        old = 1 - cur
        t0, k0 = nxt[cur]
        stage0(t0, k0, cur)
        stage3(*b3, old)
        held = stage1_head(old)
        settled = stage2(*b2, cur)
        stage1_tail(old, *held)
        moved = advance(t0, k0, b2[0], settled, t_end[cur])
        nxt = (moved, nxt[1]) if cur == 0 else (nxt[0], moved)
        ended = (k0 == nq).astype(jnp.int32)
        n_ended = (n_ended[0] + ended, n_ended[1]) if cur == 0 else (n_ended[0], n_ended[1] + ended)
        return nxt, ((t0, k0), b1, b2), n_ended

    def trip(state):
        return step(1, step(0, state))

    i32 = lambda x: jnp.int32(x)
    idle = ((i32(half - 1), i32(nq)), (i32(nq - 1), i32(nq)))
    start = ((i32(0), i32(0)), (i32(half), i32(half)))
    lax.while_loop(lambda s: (s[2][0] < 2) | (s[2][1] < 2), trip,
                   (start, (idle[1], idle[0], idle[1]), (i32(0), i32(0))))


def _sb_attn(q, kv):
    bsz, seq, c = q.shape
    tq = ATTN_TILE
    assert seq % (2 * tq) == 0
    nhb = c // LANES
    qspec = pl.BlockSpec((1, seq, LANES), lambda b, h: (b, 0, h))
    vspec = pl.BlockSpec((1, seq, LANES), lambda b, h: (b, 0, nhb + h))
    blk = (HEADS_PER_BLOCK, tq, tq)
    slot = [pltpu.VMEM(blk, F32), pltpu.VMEM(blk, F32),
            pltpu.VMEM((HEADS_PER_BLOCK, tq, LANES), F32), pltpu.VMEM(blk, BF16)]
    return pl.pallas_call(
        functools.partial(_sb_attn_kernel, tq=tq, nq=seq // tq),
        out_shape=jax.ShapeDtypeStruct((bsz, seq, c), BF16),
        grid=(bsz, nhb),
        in_specs=[qspec, qspec, vspec],
        out_specs=qspec,
        scratch_shapes=[pltpu.VMEM((2, HEADS_PER_BLOCK, tq, LANES), F32), pltpu.VMEM((2, tq, LANES), F32),
                        pltpu.VMEM((2, tq, tq), F32),
                        pltpu.VMEM((HEADS_PER_BLOCK, seq, LANES), BF16),
                        pltpu.VMEM((HEADS_PER_BLOCK, seq + tq, LANES), BF16)] + slot + slot,
        compiler_params=_params("parallel", "parallel"),
        name="sb_attn",
    )(q, kv, kv)


def _router_params(w_group, b_group, w_inner, b_inner):
    c = w_group.shape[0]
    w = jnp.concatenate([w_group, w_inner.reshape(c, N_EXPERTS)], axis=1)
    b = jnp.concatenate([b_group, b_inner.reshape(N_EXPERTS)])
    pad = LANES - w.shape[1]
    return jnp.pad(w, ((0, 0), (0, pad))).T, jnp.pad(b, (0, pad)).reshape(LANES, 1)


def kernel(x, norm_mix, norm_ffn, norm_kv, norm_final, rw_mix, rw_w_rkv, rw_w0, rw_w1, rw_w2, rw_a0, rw_a1, rw_a2, rw_g1, rw_g2, rw_k_k, rw_k_a, rw_r_k, rw_lnx_w, rw_lnx_b, rw_w_out, w_kv, sb_w_q, sb_w_out, moe_w_group, moe_b_group, moe_w_inner, moe_b_inner, moe_w_gate, moe_w_up, moe_w_down):
    bsz, seq, c = x.shape
    depth = norm_mix.shape[0]
    n_a = rw_mix.shape[0]
    m = bsz * seq
    vec = lambda t: t.reshape(1, c)
    bf = lambda t: t.astype(BF16)

    h = x.reshape(m, c)
    kv_sh = q_next = None
    q_scale = LOG2E * HEAD_DIM ** -0.5
    for l in range(depth):
        if l < n_a:
            i = l
            r, k, v, lw, kk, a, g = _rwkv_pre(
                h, seq, vec(norm_mix[l]), jnp.pad(rw_mix[i], ((0, 2), (0, 0))), bf(rw_w_rkv[i]),
                vec(rw_w0[i]), bf(rw_w1[i]), bf(rw_w2[i]), vec(rw_a0[i]), bf(rw_a1[i]), bf(rw_a2[i]),
                bf(rw_g1[i]), bf(rw_g2[i]), vec(rw_k_k[i]), vec(rw_k_a[i]))
            s3 = lambda t: t.reshape(bsz, seq, c)
            y = _wkv(s3(r), s3(lw), s3(k), s3(v), s3(kk), s3(a), vec(rw_r_k[i]),
                     vec(rw_lnx_w[i]), vec(rw_lnx_b[i]))
            h = _proj_res(y.reshape(m, c), g, bf(rw_w_out[i]), h)
        else:
            j = l - n_a
            if q_next is None:
                q_next, = _norm_proj(h, [vec(norm_mix[l])], [bf(sb_w_q[j])], [q_scale])
            q, q_next = q_next, None
            o = _sb_attn(q.reshape(bsz, seq, c), kv_sh)
            h = _proj_res(o.reshape(m, c), None, bf(sb_w_out[j]), h)
        w_r, b_r = _router_params(moe_w_group[l], moe_b_group[l], moe_w_inner[l], moe_b_inner[l])
        h = _moe(h, vec(norm_ffn[l]), w_r, b_r, bf(moe_w_gate[l]), bf(moe_w_up[l]), bf(moe_w_down[l]),
                 vec(norm_final), l == depth - 1)
        if l == n_a - 1:
            gains, weights, scales = [vec(norm_kv)], [bf(w_kv)], [1.0]
            if l + 1 < depth:
                gains.append(vec(norm_mix[l + 1]))
                weights.append(bf(sb_w_q[0]))
                scales.append(q_scale)
            outs = _norm_proj(h, gains, weights, scales)
            kv_sh = outs[0].reshape(bsz, seq, 2 * c)
            q_next = outs[1] if len(outs) > 1 else None
    return h.reshape(bsz, seq, c)
```
